```python
import jax, jax.numpy as jnp
from jax import lax
import numpy as np

D_MODEL = 1024
BATCH = 4
SEQ = 4096
DEPTH = 2

GRID_W = 64
CTX_LEN = 256

HEAD_DIM = 64
ROPE_THETA = 10000.0

ATTN_HEADS = 8
ATTN_KV_HEADS = 2
ATTN_GROUP = ATTN_HEADS // ATTN_KV_HEADS
Q_BLOCK = 128

GLA_HEADS = 4
GLA_DK = 64
GLA_DV = 128
GLA_GATE_RANK = 16
GLA_TAU = 16.0
GLA_CHUNK = 64

NA_HEADS = 8
NA_KH = 8
NA_KW = 16

A_Q = ATTN_HEADS * HEAD_DIM
A_KV = ATTN_KV_HEADS * HEAD_DIM
B_QK = GLA_HEADS * GLA_DK
B_V = GLA_HEADS * GLA_DV
B_A = 2 * GLA_GATE_RANK
C_W = NA_HEADS * HEAD_DIM
N_BRANCH = 3
IN_SPLITS = (A_Q, A_KV, A_KV, B_QK, B_QK, B_V, B_V, B_A, C_W, C_W, C_W, N_BRANCH * D_MODEL)
N_IN = sum(IN_SPLITS)
N_MOD = 6 * D_MODEL

N_GROUPS = 4
EXPERTS_PER_GROUP = 8
N_EXPERTS = N_GROUPS * EXPERTS_PER_GROUP
TOP_K = 2
D_EXPERT = 512
MOE_BLOCK = 128

LN_EPS = 1e-6
RMS_EPS = 1e-6
DEEPNORM_ALPHA = (2 * DEPTH) ** 0.25
DEEPNORM_BETA = (8 * DEPTH) ** -0.25

kernel_name = "hybrid_gated_mixers_hmoe_dit"


def _f32(a):
    return a.astype(jnp.float32)


def layer_norm0(x):
    xf = _f32(x)
    mu = xf.mean(-1, keepdims=True)
    var = jnp.square(xf - mu).mean(-1, keepdims=True)
    return ((xf - mu) * lax.rsqrt(var + LN_EPS)).astype(x.dtype)


def layer_norm(x, g, b):
    return layer_norm0(x) * g + b


def rms_norm(x, g):
    xf = _f32(x)
    return (xf * lax.rsqrt(jnp.mean(xf * xf, -1, keepdims=True) + RMS_EPS)).astype(x.dtype) * g


def modulate(x, shift, scale):
    return layer_norm0(x) * (1.0 + scale) + shift


def split_cols(z):
    idx = [int(i) for i in np.cumsum(IN_SPLITS)[:-1]]
    return jnp.split(z, idx, axis=-1)


def axial_rope(L):
    t = jnp.arange(L)
    row = (t // GRID_W).astype(jnp.float32)
    col = (t % GRID_W).astype(jnp.float32)
    n_freq = HEAD_DIM // 4
    inv = ROPE_THETA ** (-jnp.arange(n_freq, dtype=jnp.float32) / n_freq)
    ar = row[:, None] * inv
    ac = col[:, None] * inv
    ang = jnp.concatenate([ar, ar, ac, ac], axis=-1)
    return jnp.cos(ang), jnp.sin(ang)


def apply_rope(x, cos, sin):
    xf = _f32(x)
    x1, x2, x3, x4 = jnp.split(xf, 4, axis=-1)
    rot = jnp.concatenate([-x2, x1, -x4, x3], axis=-1)
    return (xf * cos[:, None, :] + rot * sin[:, None, :]).astype(x.dtype)


def softmax_attend(q, k, v):
    s = _f32(jnp.einsum('bqkgd,bskd->bkgqs', q, k)) * (HEAD_DIM ** -0.5)
    p = jax.nn.softmax(s, axis=-1).astype(v.dtype)
    return jnp.einsum('bkgqs,bskd->bqkgd', p, v)


def gqa_axial(q_lat, k_lat, v_lat, q_ctx, k_ctx, v_ctx, q_norm, k_norm, with_ctx_out):
    B, L = q_lat.shape[:2]
    Lc = q_ctx.shape[1]
    cos, sin = axial_rope(L)
    q = apply_rope(rms_norm(q_lat.reshape(B, L, ATTN_HEADS, HEAD_DIM), q_norm), cos, sin)
    k = apply_rope(rms_norm(k_lat.reshape(B, L, ATTN_KV_HEADS, HEAD_DIM), k_norm), cos, sin)
    k_c = rms_norm(k_ctx.reshape(B, Lc, ATTN_KV_HEADS, HEAD_DIM), k_norm)
    v_c = v_ctx.reshape(B, Lc, ATTN_KV_HEADS, HEAD_DIM)
    k_all = jnp.concatenate([k_c, k], axis=1)
    v_all = jnp.concatenate([v_c, v_lat.reshape(B, L, ATTN_KV_HEADS, HEAD_DIM)], axis=1)
    n_blk = L // Q_BLOCK
    q_blocks = q.reshape(B, n_blk, Q_BLOCK, ATTN_KV_HEADS, ATTN_GROUP, HEAD_DIM).transpose(1, 0, 2, 3, 4, 5)
    o = lax.map(lambda qb: softmax_attend(qb, k_all, v_all), q_blocks)
    o_lat = o.transpose(1, 0, 2, 3, 4, 5).reshape(B, L, A_Q)
    if not with_ctx_out:
        return o_lat, None
    q_c = rms_norm(q_ctx.reshape(B, Lc, ATTN_KV_HEADS, ATTN_GROUP, HEAD_DIM), q_norm)
    return o_lat, softmax_attend(q_c, k_c, v_c).reshape(B, Lc, A_Q)


def gla_chunk_scan(q, k, v, log_g, s0):
    B, H, T, dk = q.shape
    dv = v.shape[-1]
    nc = T // GLA_CHUNK

    def to_chunks(a):
        return jnp.moveaxis(a.reshape(B, H, nc, GLA_CHUNK, a.shape[-1]), 2, 0)

    causal = jnp.tril(jnp.ones((GLA_CHUNK, GLA_CHUNK), dtype=bool))[:, :, None]

    def step(S, inp):
        qc, kc, vc, gc = (_f32(a) for a in inp)
        b = jnp.cumsum(gc, axis=2)
        o_inter = jnp.einsum('bhtd,bhde->bhte', qc * jnp.exp(b), S)
        diff = b[:, :, :, None, :] - b[:, :, None, :, :]
        decay = jnp.where(causal, jnp.exp(jnp.where(causal, diff, 0.0)), 0.0)
        scores = jnp.einsum('bhtsd,bhsd->bhts', qc[:, :, :, None, :] * decay, kc)
        o_intra = jnp.einsum('bhts,bhse->bhte', scores, vc)
        b_end = b[:, :, -1:, :]
        S_new = jnp.exp(b_end[:, :, 0, :])[..., None] * S + jnp.einsum('bhsd,bhse->bhde', kc * jnp.exp(b_end - b), vc)
        return S_new, o_inter + o_intra

    S, o = lax.scan(step, s0, (to_chunks(q), to_chunks(k), to_chunks(v), to_chunks(log_g)))
    o = jnp.moveaxis(o, 0, 2).reshape(B, H, T, dv).astype(v.dtype)
    return S, o


def gla_bidir(z_lat, z_ctx, w_gate, b_gate, norm_g, with_ctx_out):
    def prep(z):
        q, k, v, r, a = z
        B, T = q.shape[:2]

        def heads(t, d):
            return t.reshape(B, T, GLA_HEADS, d).transpose(0, 2, 1, 3)

        def log_gate(a_dir, d):
            return heads(jax.nn.log_sigmoid(_f32(a_dir @ w_gate[d] + b_gate[d])) / GLA_TAU, GLA_DK)

        a_f, a_b = jnp.split(a, 2, axis=-1)
        return (heads(q, GLA_DK) * (GLA_DK ** -0.5), heads(k, GLA_DK), heads(v, GLA_DV), r,
                log_gate(a_f, 0), log_gate(a_b, 1))

    q_l, k_l, v_l, r_l, gf_l, gb_l = prep(z_lat)
    q_c, k_c, v_c, r_c, gf_c, gb_c = prep(z_ctx)
    B = q_l.shape[0]
    s0 = jnp.zeros((B, GLA_HEADS, GLA_DK, GLA_DV), jnp.float32)

    def flip(t):
        return jnp.flip(t, axis=2)

    s_cf, o_cf = gla_chunk_scan(q_c, k_c, v_c, gf_c, s0)
    _, o_lf = gla_chunk_scan(q_l, k_l, v_l, gf_l, s_cf)
    s_cb, o_cb = gla_chunk_scan(flip(q_c), flip(k_c), flip(v_c), flip(gb_c), s0)
    _, o_lb = gla_chunk_scan(flip(q_l), flip(k_l), flip(v_l), flip(gb_l), s_cb)

    def readout(o, r):
        Bo, H, T, dv = o.shape
        o = rms_norm(o.transpose(0, 2, 1, 3), norm_g).reshape(Bo, T, H * dv)
        return o * jax.nn.silu(r)

    o_lat = readout(o_lf + flip(o_lb), r_l)
    o_ctx = readout(o_cf + flip(o_cb), r_c) if with_ctx_out else None
    return o_lat, o_ctx


def neighbourhood_attn(q_lat, k_lat, v_lat, q_ctx, k_ctx, v_ctx, rpb, with_ctx_out):
    B, L = q_lat.shape[:2]
    Lc = q_ctx.shape[1]
    rows = L // GRID_W
    kh = min(NA_KH, rows)
    scale = HEAD_DIM ** -0.5

    def grid(t):
        return t.reshape(B, rows, GRID_W, NA_HEADS, HEAD_DIM).transpose(0, 3, 1, 2, 4)

    q_rows = q_lat.reshape(B, rows, GRID_W, NA_HEADS, HEAD_DIM).transpose(1, 0, 3, 2, 4)
    k_grid, v_grid = grid(k_lat), grid(v_lat)
    k_c = k_ctx.reshape(B, Lc, NA_HEADS, HEAD_DIM)
    v_c = v_ctx.reshape(B, Lc, NA_HEADS, HEAD_DIM)

    r = np.arange(rows)
    row_start = np.clip(r - kh // 2, 0, rows - kh)
    cidx = np.arange(GRID_W)
    col_start = np.clip(cidx - NA_KW // 2, 0, GRID_W - NA_KW)
    col_idx = col_start[:, None] + np.arange(NA_KW)
    row_rel = row_start[:, None] + np.arange(kh) - r[:, None] + (NA_KH - 1)
    col_rel = col_idx - cidx[:, None] + (NA_KW - 1)
    bias = rpb[:, row_rel][:, :, :, col_rel].transpose(1, 0, 3, 2, 4)

    def row_fn(inp):
        qr, rs, br = inp
        kb = lax.dynamic_slice_in_dim(k_grid, rs, kh, axis=2)
        vb = lax.dynamic_slice_in_dim(v_grid, rs, kh, axis=2)
        kg = kb[:, :, :, col_idx]
        vg = vb[:, :, :, col_idx]
        s_nb = _f32(jnp.einsum('bhwd,bhiwjd->bhwij', qr, kg)) * scale + br
        s_cx = _f32(jnp.einsum('bhwd,bchd->bhwc', qr, k_c)) * scale
        s = jnp.concatenate([s_nb.reshape(B, NA_HEADS, GRID_W, kh * NA_KW), s_cx], axis=-1)
        p = jax.nn.softmax(s, axis=-1).astype(vg.dtype)
        p_nb = p[..., :kh * NA_KW].reshape(B, NA_HEADS, GRID_W, kh, NA_KW)
        p_cx = p[..., kh * NA_KW:]
        return jnp.einsum('bhwij,bhiwjd->bhwd', p_nb, vg) + jnp.einsum('bhwc,bchd->bhwd', p_cx, v_c)

    o = lax.map(row_fn, (q_rows, jnp.asarray(row_start, jnp.int32), bias))
    o_lat = o.transpose(1, 0, 3, 2, 4).reshape(B, L, C_W)
    if not with_ctx_out:
        return o_lat, None
    q_c = q_ctx.reshape(B, Lc, NA_HEADS, 1, HEAD_DIM)
    return o_lat, softmax_attend(q_c, k_c, v_c).reshape(B, Lc, C_W)


def token_mixers(h_lat, h_ctx, w_in, b_in, q_norm, k_norm, gla_w_gate, gla_b_gate, gla_norm, na_rpb,
                 w_br_attn, w_br_gla, w_br_na, w_out, with_ctx_out):
    z_lat = split_cols(h_lat @ w_in + b_in)
    z_ctx = split_cols(h_ctx @ w_in + b_in)
    oa_l, oa_c = gqa_axial(*z_lat[0:3], *z_ctx[0:3], q_norm, k_norm, with_ctx_out)
    ob_l, ob_c = gla_bidir(z_lat[3:8], z_ctx[3:8], gla_w_gate, gla_b_gate, gla_norm, with_ctx_out)
    oc_l, oc_c = neighbourhood_attn(*z_lat[8:11], *z_ctx[8:11], na_rpb, with_ctx_out)

    def merge(o_a, o_b, o_c, gate_cols):
        g = jax.nn.sigmoid(gate_cols.reshape(gate_cols.shape[:-1] + (N_BRANCH, D_MODEL)))
        y = g[..., 0, :] * (o_a @ w_br_attn) + g[..., 1, :] * (o_b @ w_br_gla) + g[..., 2, :] * (o_c @ w_br_na)
        return y @ w_out

    y_lat = merge(oa_l, ob_l, oc_l, z_lat[11])
    y_ctx = merge(oa_c, ob_c, oc_c, z_ctx[11]) if with_ctx_out else None
    return y_lat, y_ctx


def hier_moe(h, w_rg, b_rg, w_re, b_re, w1, w3, w2):
    T, D = h.shape
    hf = _f32(h)
    g_logits = hf @ _f32(w_rg) + _f32(b_rg)
    grp = jnp.argmax(g_logits, axis=-1)
    p_grp = jax.nn.softmax(g_logits, axis=-1).max(-1, keepdims=True)
    e_logits = (hf @ _f32(w_re) + _f32(b_re)).reshape(T, N_GROUPS, EXPERTS_PER_GROUP)
    sel = jnp.einsum('tge,tg->te', e_logits, jax.nn.one_hot(grp, N_GROUPS, dtype=jnp.float32))
    top_v, top_i = lax.top_k(sel, TOP_K)
    gate = p_grp * jax.nn.softmax(top_v, axis=-1)
    expert = grp[:, None] * EXPERTS_PER_GROUP + top_i

    n_assign = T * TOP_K
    n_blocks = -(-n_assign // MOE_BLOCK) + N_EXPERTS
    flat_e = expert.reshape(-1)
    flat_t = jnp.repeat(jnp.arange(T), TOP_K)
    order = jnp.argsort(flat_e)
    se, st, sw = flat_e[order], flat_t[order], gate.reshape(-1)[order]
    counts = jnp.bincount(flat_e, length=N_EXPERTS)
    padded = ((counts + MOE_BLOCK - 1) // MOE_BLOCK) * MOE_BLOCK
    pad_end = jnp.cumsum(padded)
    pad_start = pad_end - padded
    start = jnp.cumsum(counts) - counts
    pos = pad_start[se] + (jnp.arange(n_assign) - start[se])
    block_expert = jnp.minimum(
        jnp.searchsorted(pad_end, jnp.arange(n_blocks) * MOE_BLOCK, side='right'), N_EXPERTS - 1)
    x_slots = jnp.zeros((n_blocks * MOE_BLOCK, D), h.dtype).at[pos].set(h[st])

    def expert_block(inp):
        xb, e = inp
        return (jax.nn.silu(xb @ w1[e]) * (xb @ w3[e])) @ w2[e]

    y_slots = lax.map(expert_block, (x_slots.reshape(n_blocks, MOE_BLOCK, D), block_expert)).reshape(-1, D)
    y = y_slots[pos] * sw[:, None].astype(h.dtype)
    return jax.ops.segment_sum(y, st, num_segments=T)


def setup_inputs(seed: int = 0) -> dict:
    key = jax.random.key(seed)
    ks = jax.random.split(key, 29)
    D = D_MODEL

    def n(k, shape, s):
        return jax.random.normal(k, shape, jnp.float32) * s

    return {
        "x": n(ks[0], (BATCH, SEQ, D), 1.0),
        "c": n(ks[1], (BATCH, D), 1.0),
        "ctx": n(ks[2], (BATCH, CTX_LEN, D), 1.0),
        "c_ctx": n(ks[3], (D,), 1.0),
        "w_mod": n(ks[4], (DEPTH, D, N_MOD), 0.5 * D ** -0.5),
        "b_mod": n(ks[5], (DEPTH, N_MOD), 0.01),
        "w_in": n(ks[6], (DEPTH, D, N_IN), D ** -0.5),
        "b_in": n(ks[7], (DEPTH, N_IN), 0.01),
        "attn_q_norm": 1.0 + n(ks[8], (DEPTH, HEAD_DIM), 0.02),
        "attn_k_norm": 1.0 + n(ks[9], (DEPTH, HEAD_DIM), 0.02),
        "gla_w_gate": n(ks[10], (DEPTH, 2, GLA_GATE_RANK, B_QK), GLA_GATE_RANK ** -0.5),
        "gla_b_gate": n(ks[11], (DEPTH, 2, B_QK), 0.1),
        "gla_norm": 1.0 + n(ks[12], (DEPTH, GLA_DV), 0.02),
        "na_rpb": n(ks[13], (DEPTH, NA_HEADS, 2 * NA_KH - 1, 2 * NA_KW - 1), 0.02),
        "w_br_attn": n(ks[14], (DEPTH, A_Q, D), DEEPNORM_BETA * A_Q ** -0.5),
        "w_br_gla": n(ks[15], (DEPTH, B_V, D), DEEPNORM_BETA * B_V ** -0.5),
        "w_br_na": n(ks[16], (DEPTH, C_W, D), DEEPNORM_BETA * C_W ** -0.5),
        "w_out": n(ks[17], (DEPTH, D, D), DEEPNORM_BETA * D ** -0.5),
        "ln1_g": 1.0 + n(ks[18], (DEPTH, D), 0.02),
        "ln1_b": n(ks[19], (DEPTH, D), 0.01),
        "w_router_group": n(ks[20], (DEPTH, D, N_GROUPS), D ** -0.5),
        "b_router_group": n(ks[21], (DEPTH, N_GROUPS), 0.01),
        "w_router_expert": n(ks[22], (DEPTH, D, N_EXPERTS), D ** -0.5),
        "b_router_expert": n(ks[23], (DEPTH, N_EXPERTS), 0.01),
        "moe_w1": n(ks[24], (DEPTH, N_EXPERTS, D, D_EXPERT), D ** -0.5),
        "moe_w3": n(ks[25], (DEPTH, N_EXPERTS, D, D_EXPERT), D ** -0.5),
        "moe_w2": n(ks[26], (DEPTH, N_EXPERTS, D_EXPERT, D), DEEPNORM_BETA * D_EXPERT ** -0.5),
        "ln2_g": 1.0 + n(ks[27], (DEPTH, D), 0.02),
        "ln2_b": n(ks[28], (DEPTH, D), 0.01),
    }


def reference(x, c, ctx, c_ctx, w_mod, b_mod, w_in, b_in, attn_q_norm, attn_k_norm, gla_w_gate, gla_b_gate,
              gla_norm, na_rpb, w_br_attn, w_br_gla, w_br_na, w_out, ln1_g, ln1_b, w_router_group,
              b_router_group, w_router_expert, b_router_expert, moe_w1, moe_w3, moe_w2, ln2_g, ln2_b):
    B, L, D = x.shape
    Lc = ctx.shape[1]
    silu_c = jax.nn.silu(c)
    silu_cc = jax.nn.silu(c_ctx)
    x_ctx = ctx
    for l in range(DEPTH):
        last = l == DEPTH - 1
        sh1, sc1, g1, sh2, sc2, g2 = jnp.split((silu_c @ w_mod[l] + b_mod[l])[:, None, :], 6, axis=-1)
        csh1, csc1, cg1, csh2, csc2, cg2 = jnp.split(silu_cc @ w_mod[l] + b_mod[l], 6, axis=-1)

        y_lat, y_ctx = token_mixers(modulate(x, sh1, sc1), modulate(x_ctx, csh1, csc1), w_in[l], b_in[l],
                                    attn_q_norm[l], attn_k_norm[l], gla_w_gate[l], gla_b_gate[l], gla_norm[l],
                                    na_rpb[l], w_br_attn[l], w_br_gla[l], w_br_na[l], w_out[l], not last)
        x = layer_norm(DEEPNORM_ALPHA * x + g1 * y_lat, ln1_g[l], ln1_b[l])

        h_lat = modulate(x, sh2, sc2).reshape(B * L, D)
        if last:
            tokens = h_lat
        else:
            x_ctx = layer_norm(DEEPNORM_ALPHA * x_ctx + cg1 * y_ctx, ln1_g[l], ln1_b[l])
            tokens = jnp.concatenate([h_lat, modulate(x_ctx, csh2, csc2).reshape(B * Lc, D)], axis=0)
        y = hier_moe(tokens, w_router_group[l], b_router_group[l], w_router_expert[l], b_router_expert[l],
                     moe_w1[l], moe_w3[l], moe_w2[l])
        x = layer_norm(DEEPNORM_ALPHA * x + g2 * y[:B * L].reshape(B, L, D), ln2_g[l], ln2_b[l])
        if not last:
            x_ctx = layer_norm(DEEPNORM_ALPHA * x_ctx + cg2 * y[B * L:].reshape(B, Lc, D), ln2_g[l], ln2_b[l])
    return x
```

```python
import functools

import jax
import jax.numpy as jnp
import numpy as np
from jax import lax
from jax.experimental import pallas as pl
from jax.experimental.pallas import tpu as pltpu

F32 = jnp.float32
BF16 = jnp.bfloat16
HIGHEST = lax.Precision.HIGHEST

GRID_W = 64
HEAD_DIM = 64
ROPE_THETA = 10000.0
ATTN_HEADS = 8
ATTN_KV_HEADS = 2
GLA_HEADS = 4
GLA_DK = 64
GLA_DV = 128
GLA_GATE_RANK = 16
GLA_TAU = 16.0
GLA_CHUNK = 64
GLA_SUB = 16
NA_HEADS = 8
NA_KH = 8
NA_KW = 16
N_GROUPS = 4
EXPERTS_PER_GROUP = 8
N_EXPERTS = N_GROUPS * EXPERTS_PER_GROUP
TOP_K = 2
LN_EPS = 1e-6
RMS_EPS = 1e-6
NEG_BIG = -1e30

LANES = 128
TM = 256
MOE_BLK = 256
VMEM_LIMIT = 56 * 1024 * 1024

A_Q, A_KV = ATTN_HEADS * HEAD_DIM, ATTN_KV_HEADS * HEAD_DIM
B_QK, B_V, B_A = GLA_HEADS * GLA_DK, GLA_HEADS * GLA_DV, 2 * GLA_GATE_RANK
C_W = NA_HEADS * HEAD_DIM
NCOL_A = A_Q + 2 * A_KV
NCOL_B = 2 * B_QK + 2 * B_V + LANES
NCOL_C = 3 * C_W


def _cparams(sem):
    return pltpu.CompilerParams(dimension_semantics=sem, vmem_limit_bytes=VMEM_LIMIT)


def _ln0(x):
    mu = jnp.mean(x, axis=-1, keepdims=True)
    xc = x - mu
    var = jnp.mean(xc * xc, axis=-1, keepdims=True)
    return xc * lax.rsqrt(var + LN_EPS)


def _modulate(x, shift, scale):
    return _ln0(x) * (1.0 + scale) + shift


def _nt(a, b):
    return lax.dot_general(a, b, (((1,), (1,)), ((), ())), preferred_element_type=F32)


def _dot(a, b, precision=None):
    return jnp.dot(a, b, preferred_element_type=F32, precision=precision)


def _mod_kernel(c_ref, w_ref, b_ref, o_ref):
    c = c_ref[...]
    s = c * jax.nn.sigmoid(c)
    o_ref[0] = _dot(s, w_ref[0]) + b_ref[0]


def _mod_call(cin, w_mod, b_mod):
    depth, d, nmod = w_mod.shape
    nb = nmod // d
    return pl.pallas_call(
        _mod_kernel,
        grid=(depth, nb),
        in_specs=[pl.BlockSpec((8, d), lambda l, n: (0, 0)),
                  pl.BlockSpec((1, d, d), lambda l, n: (l, 0, n)),
                  pl.BlockSpec((1, 1, d), lambda l, n: (l, 0, n))],
        out_specs=pl.BlockSpec((1, 8, d), lambda l, n: (l, 0, n)),
        out_shape=jax.ShapeDtypeStruct((depth, 8, nmod), F32),
        compiler_params=_cparams(("arbitrary", "arbitrary")),
        name="adaln_mod",
    )(cin, w_mod, b_mod.reshape(depth, 1, nmod))


def _proj_kernel(x_ref, mod_ref, w_ref, b_ref, za_ref, zb_ref, zc_ref):
    h = _modulate(x_ref[...], mod_ref[0, 0:1, :], mod_ref[0, 1:2, :]).astype(BF16)
    lo = 0
    for o_ref in (za_ref, zb_ref, zc_ref):
        hi = lo + o_ref.shape[1]
        acc = _dot(h, w_ref[:, lo:hi]) + b_ref[:, lo:hi]
        o_ref[...] = acc.astype(o_ref.dtype)
        lo = hi


def _mod_index(nt_b):
    return lambda j: ((j // nt_b) * 2 + jnp.minimum(j % nt_b, 1), 0, 0)


def _proj_call(xs, mods, w_p, b_p, nt_b):
    t, d = xs.shape
    ncol = w_p.shape[1]
    return pl.pallas_call(
        _proj_kernel,
        grid=(t // TM,),
        in_specs=[pl.BlockSpec((TM, d), lambda j: (j, 0)),
                  pl.BlockSpec((1, 6, d), _mod_index(nt_b)),
                  pl.BlockSpec((d, ncol), lambda j: (0, 0)),
                  pl.BlockSpec((1, ncol), lambda j: (0, 0))],
        out_specs=[pl.BlockSpec((TM, NCOL_A), lambda j: (j, 0)),
                   pl.BlockSpec((TM, NCOL_B), lambda j: (j, 0)),
                   pl.BlockSpec((TM, NCOL_C), lambda j: (j, 0))],
        out_shape=[jax.ShapeDtypeStruct((t, NCOL_A), F32),
                   jax.ShapeDtypeStruct((t, NCOL_B), F32),
                   jax.ShapeDtypeStruct((t, NCOL_C), BF16)],
        compiler_params=_cparams(("arbitrary",)),
        name="in_proj",
    )(xs, mods, w_p, b_p)


def _gqa_kernel(q_ref, k_ref, v_ref, cq_ref, saq_ref, sbq_ref, ck_ref, sak_ref, sbk_ref,
                qn_ref, kn_ref, gm_ref, o_ref, k2_s, v2_s, *, n_keys, ctx_len, with_ctx):
    kvh = pl.program_id(1)
    qi = pl.program_id(2)
    lane = lax.broadcasted_iota(jnp.int32, (1, LANES), 1)
    lo_half = lane < HEAD_DIM
    gm = gm_ref[...]

    def rms(x, g):
        x2 = x * x
        hi = x2.astype(BF16)
        lo = (x2 - hi.astype(F32)).astype(BF16)
        ms = _dot(hi, gm) + _dot(lo, gm)
        return x * lax.rsqrt(ms + RMS_EPS) * g

    def rope(x, c, sa, sb):
        q4 = HEAD_DIM // 4
        return x * c + pltpu.roll(x, q4, 1) * sa + pltpu.roll(x, LANES - q4, 1) * sb

    @pl.when(qi == 0)
    def _prep_keys():
        own_half = jnp.logical_xor(lo_half, kvh == 1)

        def body(c, carry):
            r0 = pl.multiple_of(c * TM, TM)
            rows = pl.ds(r0, TM)
            kk = rope(rms(k_ref[rows, :], kn_ref[...]), ck_ref[rows, :], sak_ref[rows, :], sbk_ref[rows, :])
            k2_s[rows, :] = jnp.where(own_half, kk, pltpu.roll(kk, HEAD_DIM, 1)).astype(BF16)
            vv = v_ref[rows, :]
            v2_s[rows, :] = jnp.where(own_half, vv, pltpu.roll(vv, HEAD_DIM, 1)).astype(BF16)
            return carry

        lax.fori_loop(0, n_keys // TM, body, 0)

    def attend(nk):
        halves = []
        for hh in range(2):
            xq = q_ref[:, hh * LANES:(hh + 1) * LANES]
            xq = rope(rms(xq, qn_ref[...]), cq_ref[...], saq_ref[...], sbq_ref[...]) * (HEAD_DIM ** -0.5)
            pair = []
            for g2 in range(2):
                msk = lo_half if g2 == 0 else jnp.logical_not(lo_half)
                qm = jnp.where(msk, xq, 0.0).astype(BF16)
                s = _nt(qm, k2_s[0:nk, :])
                m = jnp.max(s, axis=-1, keepdims=True)
                p = jnp.exp(s - m)
                l = jnp.sum(p, axis=-1, keepdims=True)
                o = _dot(p.astype(BF16), v2_s[0:nk, :])
                pair.append(o / l)
            halves.append(jnp.where(lo_half, pair[0], pair[1]))
        o_ref[...] = jnp.concatenate(halves, axis=1).astype(o_ref.dtype)

    if with_ctx:
        pl.when(qi == 0)(lambda: attend(ctx_len))
        pl.when(qi > 0)(lambda: attend(n_keys))
    else:
        attend(n_keys)


def _gqa_call(za, tabs, q_norm, k_norm, gm, bsz, tb, with_ctx):
    t = za.shape[0]
    nt_b = tb // TM
    q_off = 0 if with_ctx else 1
    n_q = nt_b - q_off
    cos_t, sa_t, sb_t = tabs
    qspec = pl.BlockSpec((TM, 2 * LANES), lambda b, h, i: (b * nt_b + i + q_off, h))
    kspec = pl.BlockSpec((tb, LANES), lambda b, h, i: (b, A_Q // LANES))
    vspec = pl.BlockSpec((tb, LANES), lambda b, h, i: (b, A_Q // LANES + 1))
    tq = pl.BlockSpec((TM, LANES), lambda b, h, i: (i + q_off, 0))
    tk = pl.BlockSpec((tb, LANES), lambda b, h, i: (0, 0))
    vec = pl.BlockSpec((1, LANES), lambda b, h, i: (0, 0))
    kern = functools.partial(_gqa_kernel, n_keys=tb, ctx_len=TM, with_ctx=with_ctx)
    return pl.pallas_call(
        kern,
        grid=(bsz, ATTN_KV_HEADS, n_q),
        in_specs=[qspec, kspec, vspec, tq, tq, tq, tk, tk, tk, vec, vec,
                  pl.BlockSpec((LANES, LANES), lambda b, h, i: (0, 0))],
        out_specs=pl.BlockSpec((TM, 2 * LANES), lambda b, h, i: (b * nt_b + i + q_off, h)),
        out_shape=jax.ShapeDtypeStruct((t, A_Q), BF16),
        scratch_shapes=[pltpu.VMEM((tb, LANES), BF16), pltpu.VMEM((tb, LANES), BF16)],
        compiler_params=_cparams(("arbitrary", "arbitrary", "arbitrary")),
        name="gqa_axial",
    )(za, za, za, cos_t, sa_t, sb_t, cos_t, sa_t, sb_t, q_norm, k_norm, gm)


def _gla_kernel(*refs, reverse):
    if reverse:
        q_ref, k_ref, v_ref, a_ref, r_ref, of_ref, wg_ref, bg_ref, ng_ref, o_ref, st_s = refs
    else:
        q_ref, k_ref, v_ref, a_ref, wg_ref, bg_ref, o_ref, st_s = refs
    C, SUB, H = GLA_CHUNK, GLA_SUB, GLA_HEADS
    nsub = C // SUB

    @pl.when(pl.program_id(1) == 0)
    def _zero_state():
        st_s[...] = jnp.zeros_like(st_s)

    row = lax.broadcasted_iota(jnp.int32, (C, C), 0)
    col = lax.broadcasted_iota(jnp.int32, (C, C), 1)
    if reverse:
        m_incl = (col >= row).astype(F32)
        m_blk = (col // SUB > row // SUB).astype(F32)
    else:
        m_incl = (col <= row).astype(F32)
        m_blk = (col // SUB < row // SUB).astype(F32)
    lane_k = lax.broadcasted_iota(jnp.int32, (1, B_QK), 1) // GLA_DK
    hmask = [lane_k == h for h in range(H)]
    head_expand = (lax.broadcasted_iota(jnp.int32, (B_QK, B_V), 0) // GLA_DK
                   == lax.broadcasted_iota(jnp.int32, (B_QK, B_V), 1) // GLA_DV).astype(BF16)
    scol = lax.broadcasted_iota(jnp.int32, (1, C), 1)
    trow = lax.broadcasted_iota(jnp.int32, (C, 1), 0) % SUB

    def stack_heads(x):
        return jnp.concatenate([jnp.where(hmask[h], x, 0.0) for h in range(H)], axis=0).astype(BF16)

    def head_diag(y, r, w):
        return jnp.concatenate([y[h * r:(h + 1) * r, h * w:(h + 1) * w] for h in range(H)], axis=1)

    st = st_s[...]
    n_chunks = q_ref.shape[0] // C
    order = range(n_chunks - 1, -1, -1) if reverse else range(n_chunks)
    for c in order:
        rows = slice(c * C, (c + 1) * C)
        q = q_ref[rows, :] * (GLA_DK ** -0.5)
        k = k_ref[rows, :]
        v = v_ref[rows, :]
        x = _dot(a_ref[rows, :], wg_ref[...], HIGHEST) + bg_ref[...]
        g = (jnp.minimum(x, 0.0) - jnp.log(1.0 + jnp.exp(-jnp.abs(x)))) * (1.0 / GLA_TAU)
        b = _dot(m_incl, g, HIGHEST)
        e = _dot(m_blk, g, HIGHEST)
        btot = b[0:1, :] if reverse else b[C - 1:C, :]
        vb = v.astype(BF16)

        o_int = _nt(stack_heads(q * jnp.exp(b)), st.astype(BF16))
        o = jnp.concatenate([o_int[h * C:(h + 1) * C, :] for h in range(H)], axis=1)

        kend = (k * jnp.exp(btot - b)).astype(BF16)
        upd = lax.dot_general(vb, kend, (((0,), (0,)), ((), ())), preferred_element_type=F32)
        new = upd[(H - 1) * GLA_DV:H * GLA_DV, :]
        for h in range(H - 2, -1, -1):
            new = jnp.where(hmask[h], upd[h * GLA_DV:(h + 1) * GLA_DV, :], new)
        st = st * jnp.exp(btot) + new

        qh = q * jnp.exp(b - e)
        o_rows = []
        for blk in range(nsub):
            first = (blk == nsub - 1) if reverse else (blk == 0)
            if first:
                o_rows.append(jnp.zeros((SUB, B_V), F32))
                continue
            rs = slice(blk * SUB, (blk + 1) * SUB)
            e_blk = e[blk * SUB:blk * SUB + 1, :]
            k_blk = (k * jnp.exp(jnp.minimum(e_blk - b, 0.0))).astype(BF16)
            sc = _nt(stack_heads(qh[rs, :]), k_blk)
            valid = (scol >= (blk + 1) * SUB) if reverse else (scol < blk * SUB)
            p = jnp.where(valid, sc, 0.0).astype(BF16)
            o_rows.append(head_diag(_dot(p, vb), SUB, GLA_DV))
        o = o + jnp.concatenate(o_rows, axis=0)

        for d in range(SUB):
            sh = (C - d) % C if reverse else d
            kd = pltpu.roll(k, sh, 0) if sh else k
            bd = pltpu.roll(b, sh, 0) if sh else b
            vd = pltpu.roll(v, sh, 0) if sh else v
            valid = (trow + d <= SUB - 1) if reverse else (trow >= d)
            term = (q * kd * jnp.exp(jnp.minimum(b - bd, 0.0))).astype(BF16)
            o = o + jnp.where(valid, _dot(term, head_expand), 0.0) * vd

        if reverse:
            ot = o + of_ref[rows, :]
            outs = []
            for h in range(H):
                oh = ot[:, h * GLA_DV:(h + 1) * GLA_DV]
                ms = jnp.mean(oh * oh, axis=-1, keepdims=True)
                outs.append(oh * lax.rsqrt(ms + RMS_EPS) * ng_ref[...])
            r = r_ref[rows, :]
            o_ref[rows, :] = (jnp.concatenate(outs, axis=1) * (r * jax.nn.sigmoid(r))).astype(o_ref.dtype)
        else:
            o_ref[rows, :] = o
    st_s[...] = st


def _gla_call(zb, o_fwd, wg, bg, ng, bsz, tb, reverse):
    t = zb.shape[0]
    nt_b = tb // TM
    if reverse:
        def blk(b, i):
            return b * nt_b + jnp.where(i == 0, 0, nt_b - i)
    else:
        def blk(b, i):
            return b * nt_b + i
    qs = pl.BlockSpec((TM, B_QK), lambda b, i: (blk(b, i), 0))
    ks = pl.BlockSpec((TM, B_QK), lambda b, i: (blk(b, i), 1))
    vs = pl.BlockSpec((TM, B_V), lambda b, i: (blk(b, i), 1))
    rsp = pl.BlockSpec((TM, B_V), lambda b, i: (blk(b, i), 2))
    asp = pl.BlockSpec((TM, LANES), lambda b, i: (blk(b, i), (2 * B_QK + 2 * B_V) // LANES))
    osp = pl.BlockSpec((TM, B_V), lambda b, i: (blk(b, i), 0))
    wsp = pl.BlockSpec((LANES, B_QK), lambda b, i: (0, 0))
    bsp = pl.BlockSpec((1, B_QK), lambda b, i: (0, 0))
    if reverse:
        in_specs = [qs, ks, vs, asp, rsp, osp, wsp, bsp, pl.BlockSpec((1, GLA_DV), lambda b, i: (0, 0))]
        args = (zb, zb, zb, zb, zb, o_fwd, wg, bg, ng)
        out_dtype = BF16
    else:
        in_specs = [qs, ks, vs, asp, wsp, bsp]
        args = (zb, zb, zb, zb, wg, bg)
        out_dtype = F32
    return pl.pallas_call(
        functools.partial(_gla_kernel, reverse=reverse),
        grid=(bsz, nt_b),
        in_specs=in_specs,
        out_specs=osp,
        out_shape=jax.ShapeDtypeStruct((t, B_V), out_dtype),
        scratch_shapes=[pltpu.VMEM((GLA_DV, B_QK), F32)],
        compiler_params=_cparams(("arbitrary", "arbitrary")),
        name="gla_bwd" if reverse else "gla_fwd",
    )(*args)


def _na_kernel(q_ref, k_ref, v_ref, bias_ref, o_ref, *, n_rows, ctx_len):
    i = pl.program_id(1)
    n_ctx_steps = ctx_len // GRID_W
    H, W = NA_HEADS, GRID_W
    lane_h = lax.broadcasted_iota(jnp.int32, (1, C_W), 1) // HEAD_DIM
    q = q_ref[...]
    zero = jnp.zeros_like(q)
    qs = jnp.concatenate([jnp.where(lane_h == h, q, zero) for h in range(H)], axis=0)
    scale = HEAD_DIM ** -0.5
    kc = k_ref[0:ctx_len, :]
    vc = v_ref[0:ctx_len, :]
    s_c = _nt(qs, kc) * scale

    def write(o):
        out = o[(H - 1) * W:H * W, :]
        for h in range(H - 2, -1, -1):
            out = jnp.where(lane_h == h, o[h * W:(h + 1) * W, :], out)
        o_ref[...] = out.astype(o_ref.dtype)

    @pl.when(i < n_ctx_steps)
    def _ctx_queries():
        m = jnp.max(s_c, axis=-1, keepdims=True)
        p = jnp.exp(s_c - m)
        l = jnp.sum(p, axis=-1, keepdims=True)
        write(_dot(p.astype(BF16), vc) / l)

    @pl.when(i >= n_ctx_steps)
    def _grid_row():
        r = i - n_ctx_steps
        rs = jnp.clip(r - NA_KH // 2, 0, n_rows - NA_KH)
        start = pl.multiple_of(ctx_len + rs * W, W)
        kw = k_ref[pl.ds(start, NA_KH * W), :]
        vw = v_ref[pl.ds(start, NA_KH * W), :]
        s_w = _nt(qs, kw) * scale + bias_ref[r - rs]
        m = jnp.maximum(jnp.max(s_w, axis=-1, keepdims=True), jnp.max(s_c, axis=-1, keepdims=True))
        p_w = jnp.exp(s_w - m)
        p_c = jnp.exp(s_c - m)
        l = jnp.sum(p_w, axis=-1, keepdims=True) + jnp.sum(p_c, axis=-1, keepdims=True)
        write((_dot(p_w.astype(BF16), vw) + _dot(p_c.astype(BF16), vc)) / l)


def _na_call(zc, bias_d, bsz, tb):
    t = zc.shape[0]
    steps = tb // GRID_W
    n_rows = (tb - TM) // GRID_W
    kern = functools.partial(_na_kernel, n_rows=n_rows, ctx_len=TM)
    return pl.pallas_call(
        kern,
        grid=(bsz, steps),
        in_specs=[pl.BlockSpec((GRID_W, C_W), lambda b, i: (b * steps + i, 0)),
                  pl.BlockSpec((tb, C_W), lambda b, i: (b, 1)),
                  pl.BlockSpec((tb, C_W), lambda b, i: (b, 2)),
                  pl.BlockSpec(bias_d.shape, lambda b, i: (0, 0, 0))],
        out_specs=pl.BlockSpec((GRID_W, C_W), lambda b, i: (b * steps + i, 0)),
        out_shape=jax.ShapeDtypeStruct((t, C_W), BF16),
        compiler_params=_cparams(("arbitrary", "arbitrary")),
        name="nbr_attn",
    )(zc, zc, zc, bias_d)


def _merge_kernel(x_ref, mod_ref, oa_ref, ob_ref, oc_ref, wg_ref, bg_ref, wa_ref, wb_ref, wc_ref,
                  wo_ref, lg_ref, lb_ref, wr_ref, br_ref, x1_ref, h2_ref, rt_ref, *, alpha):
    d = x_ref.shape[1]
    x = x_ref[...]
    h = _modulate(x, mod_ref[0, 0:1, :], mod_ref[0, 1:2, :]).astype(BF16)
    y = jnp.zeros_like(x)
    for bi, (o_ref, w_ref) in enumerate(((oa_ref, wa_ref), (ob_ref, wb_ref), (oc_ref, wc_ref))):
        gate = jax.nn.sigmoid(_dot(h, wg_ref[:, bi * d:(bi + 1) * d]) + bg_ref[:, bi * d:(bi + 1) * d])
        y = y + gate * _dot(o_ref[...], w_ref[...])
    y2 = _dot(y.astype(BF16), wo_ref[...])
    x1 = _ln0(alpha * x + mod_ref[0, 2:3, :] * y2) * lg_ref[...] + lb_ref[...]
    x1_ref[...] = x1
    h2 = _modulate(x1, mod_ref[0, 3:4, :], mod_ref[0, 4:5, :])
    h2_ref[...] = h2

    logits = _dot(h2, wr_ref[...], HIGHEST) + br_ref[...]
    lane = lax.broadcasted_iota(jnp.int32, logits.shape, 1)
    big = jnp.int32(1 << 20)
    neg = jnp.float32(-jnp.inf)

    def amax(vals):
        m = jnp.max(vals, axis=-1, keepdims=True)
        return m, jnp.min(jnp.where(vals == m, lane, big), axis=-1, keepdims=True)

    gl = jnp.where(lane < N_GROUPS, logits, neg)
    gmax, grp = amax(gl)
    p_grp = 1.0 / jnp.sum(jnp.exp(gl - gmax), axis=-1, keepdims=True)
    lo = N_GROUPS + grp * EXPERTS_PER_GROUP
    sl = jnp.where((lane >= lo) & (lane < lo + EXPERTS_PER_GROUP), logits, neg)
    v1, i1 = amax(sl)
    v2, i2 = amax(jnp.where(lane == i1, neg, sl))
    e2 = jnp.exp(v2 - v1)
    g1 = p_grp / (1.0 + e2)
    g2 = p_grp * e2 / (1.0 + e2)
    rt = jnp.where(lane == 0, (i1 - N_GROUPS).astype(F32),
                   jnp.where(lane == 1, (i2 - N_GROUPS).astype(F32),
                             jnp.where(lane == 2, g1, jnp.where(lane == 3, g2, 0.0))))
    rt_ref[...] = rt


def _merge_call(xs, mods, oa, ob, oc, wgate, bgate, wa, wb, wc, wo, lg, lb, wr, br, nt_b, alpha):
    t, d = xs.shape
    full = lambda shape: pl.BlockSpec(shape, lambda j: tuple(0 for _ in shape))
    row = lambda w: pl.BlockSpec((TM, w), lambda j: (j, 0))
    return pl.pallas_call(
        functools.partial(_merge_kernel, alpha=alpha),
        grid=(t // TM,),
        in_specs=[row(d), pl.BlockSpec((1, 6, d), _mod_index(nt_b)), row(A_Q), row(B_V), row(C_W),
                  full(wgate.shape), full(bgate.shape), full(wa.shape), full(wb.shape), full(wc.shape),
                  full(wo.shape), full(lg.shape), full(lb.shape), full(wr.shape), full(br.shape)],
        out_specs=[row(d), row(d), row(LANES)],
        out_shape=[jax.ShapeDtypeStruct((t, d), F32), jax.ShapeDtypeStruct((t, d), F32),
                   jax.ShapeDtypeStruct((t, LANES), F32)],
        compiler_params=_cparams(("arbitrary",)),
        name="merge_router",
    )(xs, mods, oa, ob, oc, wgate, bgate, wa, wb, wc, wo, lg, lb, wr, br)


def _dispatch_kernel(pos_ref, h_ref, xs_in_ref, xs_ref, sem):
    del xs_in_ref

    def row_copy(tok, slot):
        return pltpu.make_async_copy(h_ref.at[pl.ds(tok, 1)], xs_ref.at[pl.ds(slot, 1)], sem)

    def issue(tok, carry):
        for kk in range(TOP_K):
            row_copy(tok, pos_ref[0, 0, TOP_K * tok + kk]).start()
        return carry

    def drain(tok, carry):
        for kk in range(TOP_K):
            row_copy(0, 0).wait()
        return carry

    lax.fori_loop(0, TM, issue, 0)
    lax.fori_loop(0, TM, drain, 0)


def _dispatch_call(pos3, h2, n_slots):
    t, d = h2.shape
    zeros = jnp.zeros((n_slots, d), F32)
    return pl.pallas_call(
        _dispatch_kernel,
        grid=(t // TM,),
        in_specs=[pl.BlockSpec((1, 1, TOP_K * TM), lambda j: (j, 0, 0), memory_space=pltpu.SMEM),
                  pl.BlockSpec((TM, d), lambda j: (j, 0)),
                  pl.BlockSpec(memory_space=pl.ANY)],
        out_specs=pl.BlockSpec(memory_space=pl.ANY),
        out_shape=jax.ShapeDtypeStruct((n_slots, d), F32),
        scratch_shapes=[pltpu.SemaphoreType.DMA(())],
        input_output_aliases={2: 0},
        compiler_params=_cparams(("arbitrary",)),
        name="moe_dispatch",
    )(pos3, h2, zeros)


def _expert_kernel(be_ref, x_ref, w1_ref, w3_ref, w2_ref, y_ref):
    del be_ref
    xb = x_ref[...].astype(BF16)
    a = _dot(xb, w1_ref[0].astype(BF16))
    b = _dot(xb, w3_ref[0].astype(BF16))
    mid = (a * jax.nn.sigmoid(a) * b).astype(BF16)
    y_ref[...] = _dot(mid, w2_ref[0].astype(BF16))


def _expert_call(block_expert, x_slots, w1, w3, w2):
    n_slots, d = x_slots.shape
    de = w1.shape[2]
    grid_spec = pltpu.PrefetchScalarGridSpec(
        num_scalar_prefetch=1,
        grid=(n_slots // MOE_BLK,),
        in_specs=[pl.BlockSpec((MOE_BLK, d), lambda j, be: (j, 0)),
                  pl.BlockSpec((1, d, de), lambda j, be: (be[j], 0, 0)),
                  pl.BlockSpec((1, d, de), lambda j, be: (be[j], 0, 0)),
                  pl.BlockSpec((1, de, d), lambda j, be: (be[j], 0, 0))],
        out_specs=pl.BlockSpec((MOE_BLK, d), lambda j, be: (j, 0)),
    )
    return pl.pallas_call(
        _expert_kernel,
        grid_spec=grid_spec,
        out_shape=jax.ShapeDtypeStruct((n_slots, d), F32),
        compiler_params=_cparams(("arbitrary",)),
        name="moe_experts",
    )(block_expert, x_slots, w1, w3, w2)


def _combine_kernel(pos_ref, x1_ref, mod_ref, rt_ref, ys_ref, lg_ref, lb_ref, o_ref, buf, sem, *, alpha):
    def row_copy(slot, kk, tok):
        return pltpu.make_async_copy(ys_ref.at[pl.ds(slot, 1)], buf.at[kk, pl.ds(tok, 1)], sem)

    def issue(tok, carry):
        for kk in range(TOP_K):
            row_copy(pos_ref[0, 0, TOP_K * tok + kk], kk, tok).start()
        return carry

    def drain(tok, carry):
        for kk in range(TOP_K):
            row_copy(0, kk, 0).wait()
        return carry

    lax.fori_loop(0, TM, issue, 0)
    lax.fori_loop(0, TM, drain, 0)
    rt = rt_ref[...]
    y = rt[:, 2:3] * buf[0] + rt[:, 3:4] * buf[1]
    x2 = _ln0(alpha * x1_ref[...] + mod_ref[0, 5:6, :] * y) * lg_ref[...] + lb_ref[...]
    o_ref[...] = x2


def _combine_call(pos3, x1, mods, rt, y_slots, lg, lb, nt_b, alpha):
    t, d = x1.shape
    vec = pl.BlockSpec((1, d), lambda j: (0, 0))
    return pl.pallas_call(
        functools.partial(_combine_kernel, alpha=alpha),
        grid=(t // TM,),
        in_specs=[pl.BlockSpec((1, 1, TOP_K * TM), lambda j: (j, 0, 0), memory_space=pltpu.SMEM),
                  pl.BlockSpec((TM, d), lambda j: (j, 0)),
                  pl.BlockSpec((1, 6, d), _mod_index(nt_b)),
                  pl.BlockSpec((TM, LANES), lambda j: (j, 0)),
                  pl.BlockSpec(memory_space=pl.ANY), vec, vec],
        out_specs=pl.BlockSpec((TM, d), lambda j: (j, 0)),
        out_shape=jax.ShapeDtypeStruct((t, d), F32),
        scratch_shapes=[pltpu.VMEM((TOP_K, TM, d), F32), pltpu.SemaphoreType.DMA(())],
        compiler_params=_cparams(("arbitrary",)),
        name="moe_combine",
    )(pos3, x1, mods, rt, y_slots, lg, lb)


def _slot_plan(rt, n_blocks):
    flat_e = rt[:, 0:TOP_K].astype(jnp.int32).reshape(-1)
    onehot = (flat_e[:, None] == jnp.arange(N_EXPERTS, dtype=jnp.int32)[None, :]).astype(jnp.int32)
    csum = jnp.cumsum(onehot, axis=0)
    counts = csum[-1]
    padded = ((counts + MOE_BLK - 1) // MOE_BLK) * MOE_BLK
    pad_end = jnp.cumsum(padded)
    pad_start = pad_end - padded
    pos = jnp.sum(onehot * (csum - 1 + pad_start[None, :]), axis=1)
    block_expert = jnp.minimum(
        jnp.searchsorted(pad_end, jnp.arange(n_blocks, dtype=jnp.int32) * MOE_BLK, side='right'),
        N_EXPERTS - 1).astype(jnp.int32)
    return pos.astype(jnp.int32), block_expert


def _rope_tables(seq_len, ctx_len):
    n_freq = HEAD_DIM // 4
    t = jnp.arange(seq_len)
    invf = ROPE_THETA ** (-jnp.arange(n_freq, dtype=F32) / n_freq)
    ar = (t // GRID_W).astype(F32)[:, None] * invf
    ac = (t % GRID_W).astype(F32)[:, None] * invf
    ang = jnp.concatenate([ar, ar, ac, ac], axis=-1)
    cos, sin = jnp.cos(ang), jnp.sin(ang)
    quarter = (np.arange(HEAD_DIM) // n_freq) % 2
    sa = sin * jnp.asarray(quarter == 1, F32)
    sb = -sin * jnp.asarray(quarter == 0, F32)
    ones = jnp.ones((ctx_len, HEAD_DIM), F32)
    zeros = jnp.zeros((ctx_len, HEAD_DIM), F32)
    tabs = []
    for lat, ctx in ((cos, ones), (sa, zeros), (sb, zeros)):
        tab = jnp.concatenate([ctx, lat], axis=0)
        tabs.append(jnp.tile(tab, (1, LANES // HEAD_DIM)))
    return tabs


def _na_bias_table(rpb, n_rows):
    kh = NA_KH
    w = np.arange(GRID_W)
    col_start = np.clip(w - NA_KW // 2, 0, GRID_W - NA_KW)
    colk = np.arange(GRID_W)
    inside = (colk[None, :] >= col_start[:, None]) & (colk[None, :] < col_start[:, None] + NA_KW)
    col_rel = np.clip(colk[None, :] - w[:, None] + (NA_KW - 1), 0, 2 * NA_KW - 2)
    pats = np.arange(kh)
    row_rel = np.arange(kh)[None, :] - pats[:, None] + (NA_KH - 1)
    b = rpb[:, row_rel][:, :, :, col_rel]
    b = jnp.where(jnp.asarray(inside)[None, None, None], b, NEG_BIG)
    b = b.transpose(1, 0, 3, 2, 4)
    return b.reshape(kh, NA_HEADS * GRID_W, kh * GRID_W).astype(F32)


def _pack_in_proj(w_in, b_in):
    a0 = 0
    a1 = a0 + NCOL_A
    bq0 = a1
    b_end = bq0 + 2 * B_QK + 2 * B_V + B_A
    c_end = b_end + NCOL_C
    d = w_in.shape[0]
    padw = jnp.zeros((d, LANES - B_A), w_in.dtype)
    padb = jnp.zeros((LANES - B_A,), b_in.dtype)
    w_p = jnp.concatenate([w_in[:, a0:a1], w_in[:, bq0:b_end], padw, w_in[:, b_end:c_end]], axis=1)
    b_p = jnp.concatenate([b_in[a0:a1], b_in[bq0:b_end], padb, b_in[b_end:c_end]])
    return w_p.astype(BF16), b_p.reshape(1, -1), w_in[:, c_end:].astype(BF16), b_in[c_end:].reshape(1, -1)


def kernel(x, c, ctx, c_ctx, w_mod, b_mod, w_in, b_in, attn_q_norm, attn_k_norm, gla_w_gate, gla_b_gate,
           gla_norm, na_rpb, w_br_attn, w_br_gla, w_br_na, w_out, ln1_g, ln1_b, w_router_group,
           b_router_group, w_router_expert, b_router_expert, moe_w1, moe_w3, moe_w2, ln2_g, ln2_b):
    bsz, seq, d = x.shape
    ctx_len = ctx.shape[1]
    depth = w_mod.shape[0]
    assert ctx_len == TM and seq % TM == 0 and (seq // GRID_W) >= NA_KH
    tb = ctx_len + seq
    nt_b = tb // TM
    t = bsz * tb
    alpha = (2 * depth) ** 0.25

    xs = jnp.concatenate([ctx, x], axis=1).reshape(t, d)

    cin = jnp.zeros((8, d), F32).at[0:bsz].set(c).at[bsz].set(c_ctx)
    mod_all = _mod_call(cin, w_mod, b_mod)

    tabs = _rope_tables(seq, ctx_len)
    gm = jnp.asarray(np.kron(np.eye(LANES // HEAD_DIM), np.ones((HEAD_DIM, HEAD_DIM))) / HEAD_DIM, BF16)
    n_assign = t * TOP_K
    n_blocks = -(-n_assign // MOE_BLK) + N_EXPERTS
    n_slots = n_blocks * MOE_BLK

    for l in range(depth):
        with_ctx = True
        m = mod_all[l].reshape(8, 6, d)
        mods = jnp.stack([jnp.broadcast_to(m[bsz], (bsz, 6, d)), m[0:bsz]], axis=1).reshape(bsz * 2, 6, d)

        w_p, b_p, w_gate_cols, b_gate_cols = _pack_in_proj(w_in[l], b_in[l])
        za, zb, zc = _proj_call(xs, mods, w_p, b_p, nt_b)

        qn = jnp.tile(attn_q_norm[l], LANES // HEAD_DIM).reshape(1, LANES)
        kn = jnp.tile(attn_k_norm[l], LANES // HEAD_DIM).reshape(1, LANES)
        oa = _gqa_call(za, tabs, qn, kn, gm, bsz, tb, with_ctx)

        o_f = None
        for direction in range(2):
            wg = jnp.zeros((LANES, B_QK), F32).at[direction * GLA_GATE_RANK:(direction + 1) * GLA_GATE_RANK].set(
                gla_w_gate[l, direction])
            bg = gla_b_gate[l, direction].reshape(1, B_QK)
            o_f = _gla_call(zb, o_f, wg, bg, gla_norm[l].reshape(1, GLA_DV), bsz, tb, reverse=direction == 1)
        ob = o_f

        oc = _na_call(zc, _na_bias_table(na_rpb[l], seq // GRID_W), bsz, tb)

        wr = jnp.concatenate([w_router_group[l], w_router_expert[l],
                              jnp.zeros((d, LANES - N_GROUPS - N_EXPERTS), F32)], axis=1)
        br = jnp.concatenate([b_router_group[l], b_router_expert[l],
                              jnp.zeros((LANES - N_GROUPS - N_EXPERTS,), F32)]).reshape(1, LANES)
        x1, h2, rt = _merge_call(xs, mods, oa, ob, oc, w_gate_cols, b_gate_cols,
                                 w_br_attn[l].astype(BF16), w_br_gla[l].astype(BF16), w_br_na[l].astype(BF16),
                                 w_out[l].astype(BF16), ln1_g[l].reshape(1, d), ln1_b[l].reshape(1, d),
                                 wr, br, nt_b, alpha)

        pos, block_expert = _slot_plan(rt, n_blocks)
        pos3 = pos.reshape(t // TM, 1, TOP_K * TM)
        x_slots = _dispatch_call(pos3, h2, n_slots)
        y_slots = _expert_call(block_expert, x_slots, moe_w1[l], moe_w3[l], moe_w2[l])
        xs = _combine_call(pos3, x1, mods, rt, y_slots, ln2_g[l].reshape(1, d), ln2_b[l].reshape(1, d),
                           nt_b, alpha)

    return xs.reshape(bsz, tb, d)[:, ctx_len:, :]
```

```python
import functools

import jax
import jax.numpy as jnp
import numpy as np
from jax import lax
from jax.experimental import pallas as pl
from jax.experimental.pallas import tpu as pltpu

F32 = jnp.float32
BF16 = jnp.bfloat16
HIGHEST = lax.Precision.HIGHEST

GRID_W = 64
HEAD_DIM = 64
ROPE_THETA = 10000.0
ATTN_HEADS = 8
ATTN_KV_HEADS = 2
GLA_HEADS = 4
GLA_DK = 64
GLA_DV = 128
GLA_GATE_RANK = 16
GLA_TAU = 16.0
GLA_CHUNK = 64
GLA_SUB = 16
NA_HEADS = 8
NA_KH = 8
NA_KW = 16
N_GROUPS = 4
EXPERTS_PER_GROUP = 8
N_EXPERTS = N_GROUPS * EXPERTS_PER_GROUP
TOP_K = 2
LN_EPS = 1e-6
RMS_EPS = 1e-6
NEG_BIG = -1e30
LOG2E = 1.4426950408889634

LANES = 128
TM = 256
MOE_BLK = 256
DMA_UNROLL = 8
VMEM_LIMIT = 56 * 1024 * 1024

A_Q, A_KV = ATTN_HEADS * HEAD_DIM, ATTN_KV_HEADS * HEAD_DIM
B_QK, B_V, B_A = GLA_HEADS * GLA_DK, GLA_HEADS * GLA_DV, 2 * GLA_GATE_RANK
C_W = NA_HEADS * HEAD_DIM
NCOL_A = A_Q + 2 * A_KV
NCOL_B = 2 * B_QK + 2 * B_V + LANES
NCOL_C = 3 * C_W


def _cparams(sem):
    return pltpu.CompilerParams(dimension_semantics=sem, vmem_limit_bytes=VMEM_LIMIT)


def _ln0(x):
    mu = jnp.mean(x, axis=-1, keepdims=True)
    xc = x - mu
    var = jnp.mean(xc * xc, axis=-1, keepdims=True)
    return xc * lax.rsqrt(var + LN_EPS)


def _modulate(x, shift, scale):
    return _ln0(x) * (1.0 + scale) + shift


def _nt(a, b):
    return lax.dot_general(a, b, (((1,), (1,)), ((), ())), preferred_element_type=F32)


def _dot(a, b, precision=None):
    return jnp.dot(a, b, preferred_element_type=F32, precision=precision)


def _mod_kernel(c_ref, w_ref, b_ref, o_ref):
    c = c_ref[...]
    s = c * jax.nn.sigmoid(c)
    o_ref[0] = _dot(s, w_ref[0]) + b_ref[0]


def _mod_call(cin, w_mod, b_mod):
    depth, d, nmod = w_mod.shape
    nb = nmod // d
    return pl.pallas_call(
        _mod_kernel,
        grid=(depth, nb),
        in_specs=[pl.BlockSpec((8, d), lambda l, n: (0, 0)),
                  pl.BlockSpec((1, d, d), lambda l, n: (l, 0, n)),
                  pl.BlockSpec((1, 1, d), lambda l, n: (l, 0, n))],
        out_specs=pl.BlockSpec((1, 8, d), lambda l, n: (l, 0, n)),
        out_shape=jax.ShapeDtypeStruct((depth, 8, nmod), F32),
        compiler_params=_cparams(("arbitrary", "arbitrary")),
        name="adaln_mod",
    )(cin, w_mod, b_mod.reshape(depth, 1, nmod))


def _proj_kernel(x_ref, mod_ref, w_ref, b_ref, za_ref, zb_ref, zc_ref):
    h = _modulate(x_ref[...], mod_ref[0, 0:1, :], mod_ref[0, 1:2, :]).astype(BF16)
    lo = 0
    for o_ref in (za_ref, zb_ref, zc_ref):
        hi = lo + o_ref.shape[1]
        acc = _dot(h, w_ref[:, lo:hi]) + b_ref[:, lo:hi]
        o_ref[...] = acc.astype(o_ref.dtype)
        lo = hi


def _mod_index(nt_b):
    return lambda j: ((j // nt_b) * 2 + jnp.minimum(j % nt_b, 1), 0, 0)


def _proj_call(xs, mods, w_p, b_p, nt_b):
    t, d = xs.shape
    ncol = w_p.shape[1]
    return pl.pallas_call(
        _proj_kernel,
        grid=(t // TM,),
        in_specs=[pl.BlockSpec((TM, d), lambda j: (j, 0)),
                  pl.BlockSpec((1, 6, d), _mod_index(nt_b)),
                  pl.BlockSpec((d, ncol), lambda j: (0, 0)),
                  pl.BlockSpec((1, ncol), lambda j: (0, 0))],
        out_specs=[pl.BlockSpec((TM, NCOL_A), lambda j: (j, 0)),
                   pl.BlockSpec((TM, NCOL_B), lambda j: (j, 0)),
                   pl.BlockSpec((TM, NCOL_C), lambda j: (j, 0))],
        out_shape=[jax.ShapeDtypeStruct((t, NCOL_A), F32),
                   jax.ShapeDtypeStruct((t, NCOL_B), F32),
                   jax.ShapeDtypeStruct((t, NCOL_C), BF16)],
        compiler_params=_cparams(("arbitrary",)),
        name="in_proj",
    )(xs, mods, w_p, b_p)


def _gqa_kernel(q_ref, k_ref, v_ref, cq_ref, saq_ref, sbq_ref, ck_ref, sak_ref, sbk_ref,
                qn_ref, kn_ref, gm_ref, o_ref, k2_s, v2_s, *, n_keys, ctx_len, with_ctx):
    kvh = pl.program_id(1)
    qi = pl.program_id(2)
    lane = lax.broadcasted_iota(jnp.int32, (1, LANES), 1)
    lo_half = lane < HEAD_DIM
    gm = gm_ref[...]

    def rms(x, g):
        x2 = x * x
        hi = x2.astype(BF16)
        lo = (x2 - hi.astype(F32)).astype(BF16)
        ms = _dot(hi, gm) + _dot(lo, gm)
        return x * lax.rsqrt(ms + RMS_EPS) * g

    def rope(x, c, sa, sb):
        q4 = HEAD_DIM // 4
        return x * c + pltpu.roll(x, q4, 1) * sa + pltpu.roll(x, LANES - q4, 1) * sb

    @pl.when(qi == 0)
    def _prep_keys():
        own_half = jnp.logical_xor(lo_half, kvh == 1)

        def body(c, carry):
            r0 = pl.multiple_of(c * TM, TM)
            rows = pl.ds(r0, TM)
            kk = rope(rms(k_ref[rows, :], kn_ref[...]), ck_ref[rows, :], sak_ref[rows, :], sbk_ref[rows, :])
            k2_s[rows, :] = jnp.where(own_half, kk, pltpu.roll(kk, HEAD_DIM, 1)).astype(BF16)
            vv = v_ref[rows, :]
            v2_s[rows, :] = jnp.where(own_half, vv, pltpu.roll(vv, HEAD_DIM, 1)).astype(BF16)
            return carry

        lax.fori_loop(0, n_keys // TM, body, 0)

    def attend(nk):
        halves = []
        for hh in range(2):
            xq = q_ref[:, hh * LANES:(hh + 1) * LANES]
            xq = rope(rms(xq, qn_ref[...]), cq_ref[...], saq_ref[...], sbq_ref[...]) * (HEAD_DIM ** -0.5 * LOG2E)
            pair = []
            for g2 in range(2):
                msk = lo_half if g2 == 0 else jnp.logical_not(lo_half)
                qm = jnp.where(msk, xq, 0.0).astype(BF16)
                s = _nt(qm, k2_s[0:nk, :])
                m = jnp.max(s, axis=-1, keepdims=True)
                p = jnp.exp2(s - m)
                l = jnp.sum(p, axis=-1, keepdims=True)
                o = _dot(p.astype(BF16), v2_s[0:nk, :])
                pair.append(o / l)
            halves.append(jnp.where(lo_half, pair[0], pair[1]))
        o_ref[...] = jnp.concatenate(halves, axis=1).astype(o_ref.dtype)

    if with_ctx:
        pl.when(qi == 0)(lambda: attend(ctx_len))
        pl.when(qi > 0)(lambda: attend(n_keys))
    else:
        attend(n_keys)


def _gqa_call(za, tabs, q_norm, k_norm, gm, bsz, tb, with_ctx):
    t = za.shape[0]
    nt_b = tb // TM
    q_off = 0 if with_ctx else 1
    n_q = nt_b - q_off
    cos_t, sa_t, sb_t = tabs
    qspec = pl.BlockSpec((TM, 2 * LANES), lambda b, h, i: (b * nt_b + i + q_off, h))
    kspec = pl.BlockSpec((tb, LANES), lambda b, h, i: (b, A_Q // LANES))
    vspec = pl.BlockSpec((tb, LANES), lambda b, h, i: (b, A_Q // LANES + 1))
    tq = pl.BlockSpec((TM, LANES), lambda b, h, i: (i + q_off, 0))
    tk = pl.BlockSpec((tb, LANES), lambda b, h, i: (0, 0))
    vec = pl.BlockSpec((1, LANES), lambda b, h, i: (0, 0))
    kern = functools.partial(_gqa_kernel, n_keys=tb, ctx_len=TM, with_ctx=with_ctx)
    return pl.pallas_call(
        kern,
        grid=(bsz, ATTN_KV_HEADS, n_q),
        in_specs=[qspec, kspec, vspec, tq, tq, tq, tk, tk, tk, vec, vec,
                  pl.BlockSpec((LANES, LANES), lambda b, h, i: (0, 0))],
        out_specs=pl.BlockSpec((TM, 2 * LANES), lambda b, h, i: (b * nt_b + i + q_off, h)),
        out_shape=jax.ShapeDtypeStruct((t, A_Q), BF16),
        scratch_shapes=[pltpu.VMEM((tb, LANES), BF16), pltpu.VMEM((tb, LANES), BF16)],
        compiler_params=_cparams(("arbitrary", "arbitrary", "arbitrary")),
        name="gqa_axial",
    )(za, za, za, cos_t, sa_t, sb_t, cos_t, sa_t, sb_t, q_norm, k_norm, gm)


def _gla_kernel(*refs, reverse):
    if reverse:
        q_ref, k_ref, v_ref, a_ref, r_ref, of_ref, wg_ref, bg_ref, ng_ref, o_ref, st_s = refs
    else:
        q_ref, k_ref, v_ref, a_ref, wg_ref, bg_ref, o_ref, st_s = refs
    C, SUB, H = GLA_CHUNK, GLA_SUB, GLA_HEADS
    nsub = C // SUB

    @pl.when(pl.program_id(1) == 0)
    def _zero_state():
        st_s[...] = jnp.zeros_like(st_s)

    row = lax.broadcasted_iota(jnp.int32, (C, C), 0)
    col = lax.broadcasted_iota(jnp.int32, (C, C), 1)
    if reverse:
        m_incl = (col >= row).astype(F32)
        m_blk = (col // SUB > row // SUB).astype(F32)
    else:
        m_incl = (col <= row).astype(F32)
        m_blk = (col // SUB < row // SUB).astype(F32)
    lane_k = lax.broadcasted_iota(jnp.int32, (1, B_QK), 1) // GLA_DK
    hmask = [lane_k == h for h in range(H)]
    head_expand = (lax.broadcasted_iota(jnp.int32, (B_QK, B_V), 0) // GLA_DK
                   == lax.broadcasted_iota(jnp.int32, (B_QK, B_V), 1) // GLA_DV).astype(BF16)
    scol = lax.broadcasted_iota(jnp.int32, (1, C), 1)
    trow = lax.broadcasted_iota(jnp.int32, (C, 1), 0) % SUB

    def stack_heads(x):
        return jnp.concatenate([jnp.where(hmask[h], x, 0.0) for h in range(H)], axis=0).astype(BF16)

    def head_diag(y, r, w):
        return jnp.concatenate([y[h * r:(h + 1) * r, h * w:(h + 1) * w] for h in range(H)], axis=1)

    st = st_s[...]
    n_chunks = q_ref.shape[0] // C
    order = range(n_chunks - 1, -1, -1) if reverse else range(n_chunks)
    for c in order:
        rows = slice(c * C, (c + 1) * C)
        q = q_ref[rows, :] * (GLA_DK ** -0.5)
        k = k_ref[rows, :]
        v = v_ref[rows, :]
        x = _dot(a_ref[rows, :], wg_ref[...], HIGHEST) + bg_ref[...]
        g = (jnp.minimum(x, 0.0) - jnp.log(1.0 + jnp.exp(-jnp.abs(x)))) * (1.0 / GLA_TAU)
        b = _dot(m_incl, g, HIGHEST)
        e = _dot(m_blk, g, HIGHEST)
        btot = b[0:1, :] if reverse else b[C - 1:C, :]
        vb = v.astype(BF16)

        o_int = _nt(stack_heads(q * jnp.exp(b)), st.astype(BF16))
        o = jnp.concatenate([o_int[h * C:(h + 1) * C, :] for h in range(H)], axis=1)

        kend = (k * jnp.exp(btot - b)).astype(BF16)
        upd = lax.dot_general(vb, kend, (((0,), (0,)), ((), ())), preferred_element_type=F32)
        new = upd[(H - 1) * GLA_DV:H * GLA_DV, :]
        for h in range(H - 2, -1, -1):
            new = jnp.where(hmask[h], upd[h * GLA_DV:(h + 1) * GLA_DV, :], new)
        st = st * jnp.exp(btot) + new

        qh = q * jnp.exp(b - e)
        o_rows = []
        for blk in range(nsub):
            first = (blk == nsub - 1) if reverse else (blk == 0)
            if first:
                o_rows.append(jnp.zeros((SUB, B_V), F32))
                continue
            rs = slice(blk * SUB, (blk + 1) * SUB)
            e_blk = e[blk * SUB:blk * SUB + 1, :]
            k_blk = (k * jnp.exp(jnp.minimum(e_blk - b, 0.0))).astype(BF16)
            sc = _nt(stack_heads(qh[rs, :]), k_blk)
            valid = (scol >= (blk + 1) * SUB) if reverse else (scol < blk * SUB)
            p = jnp.where(valid, sc, 0.0).astype(BF16)
            o_rows.append(head_diag(_dot(p, vb), SUB, GLA_DV))
        o = o + jnp.concatenate(o_rows, axis=0)

        for d in range(SUB):
            sh = (C - d) % C if reverse else d
            kd = pltpu.roll(k, sh, 0) if sh else k
            bd = pltpu.roll(b, sh, 0) if sh else b
            vd = pltpu.roll(v, sh, 0) if sh else v
            valid = (trow + d <= SUB - 1) if reverse else (trow >= d)
            term = (q * kd * jnp.exp(jnp.minimum(b - bd, 0.0))).astype(BF16)
            o = o + jnp.where(valid, _dot(term, head_expand), 0.0) * vd

        if reverse:
            ot = o + of_ref[rows, :]
            outs = []
            for h in range(H):
                oh = ot[:, h * GLA_DV:(h + 1) * GLA_DV]
                ms = jnp.mean(oh * oh, axis=-1, keepdims=True)
                outs.append(oh * lax.rsqrt(ms + RMS_EPS) * ng_ref[...])
            r = r_ref[rows, :]
            o_ref[rows, :] = (jnp.concatenate(outs, axis=1) * (r * jax.nn.sigmoid(r))).astype(o_ref.dtype)
        else:
            o_ref[rows, :] = o
    st_s[...] = st


def _gla_call(zb, o_fwd, wg, bg, ng, bsz, tb, reverse):
    t = zb.shape[0]
    nt_b = tb // TM
    if reverse:
        def blk(b, i):
            return b * nt_b + jnp.where(i == 0, 0, nt_b - i)
    else:
        def blk(b, i):
            return b * nt_b + i
    qs = pl.BlockSpec((TM, B_QK), lambda b, i: (blk(b, i), 0))
    ks = pl.BlockSpec((TM, B_QK), lambda b, i: (blk(b, i), 1))
    vs = pl.BlockSpec((TM, B_V), lambda b, i: (blk(b, i), 1))
    rsp = pl.BlockSpec((TM, B_V), lambda b, i: (blk(b, i), 2))
    asp = pl.BlockSpec((TM, LANES), lambda b, i: (blk(b, i), (2 * B_QK + 2 * B_V) // LANES))
    osp = pl.BlockSpec((TM, B_V), lambda b, i: (blk(b, i), 0))
    wsp = pl.BlockSpec((LANES, B_QK), lambda b, i: (0, 0))
    bsp = pl.BlockSpec((1, B_QK), lambda b, i: (0, 0))
    if reverse:
        in_specs = [qs, ks, vs, asp, rsp, osp, wsp, bsp, pl.BlockSpec((1, GLA_DV), lambda b, i: (0, 0))]
        args = (zb, zb, zb, zb, zb, o_fwd, wg, bg, ng)
        out_dtype = BF16
    else:
        in_specs = [qs, ks, vs, asp, wsp, bsp]
        args = (zb, zb, zb, zb, wg, bg)
        out_dtype = F32
    return pl.pallas_call(
        functools.partial(_gla_kernel, reverse=reverse),
        grid=(bsz, nt_b),
        in_specs=in_specs,
        out_specs=osp,
        out_shape=jax.ShapeDtypeStruct((t, B_V), out_dtype),
        scratch_shapes=[pltpu.VMEM((GLA_DV, B_QK), F32)],
        compiler_params=_cparams(("arbitrary", "arbitrary")),
        name="gla_bwd" if reverse else "gla_fwd",
    )(*args)


def _na_kernel(q_ref, k_ref, v_ref, bias_ref, o_ref, *, n_rows, ctx_len):
    i = pl.program_id(1)
    n_ctx_steps = ctx_len // GRID_W
    H, W = NA_HEADS, GRID_W
    lane_h = lax.broadcasted_iota(jnp.int32, (1, C_W), 1) // HEAD_DIM
    q = q_ref[...]
    zero = jnp.zeros_like(q)
    qs = jnp.concatenate([jnp.where(lane_h == h, q, zero) for h in range(H)], axis=0)
    scale = HEAD_DIM ** -0.5
    kc = k_ref[0:ctx_len, :]
    vc = v_ref[0:ctx_len, :]
    s_c = _nt(qs, kc) * scale

    def write(o):
        out = o[(H - 1) * W:H * W, :]
        for h in range(H - 2, -1, -1):
            out = jnp.where(lane_h == h, o[h * W:(h + 1) * W, :], out)
        o_ref[...] = out.astype(o_ref.dtype)

    @pl.when(i < n_ctx_steps)
    def _ctx_queries():
        m = jnp.max(s_c, axis=-1, keepdims=True)
        p = jnp.exp(s_c - m)
        l = jnp.sum(p, axis=-1, keepdims=True)
        write(_dot(p.astype(BF16), vc) / l)

    @pl.when(i >= n_ctx_steps)
    def _grid_row():
        r = i - n_ctx_steps
        rs = jnp.clip(r - NA_KH // 2, 0, n_rows - NA_KH)
        start = pl.multiple_of(ctx_len + rs * W, W)
        kw = k_ref[pl.ds(start, NA_KH * W), :]
        vw = v_ref[pl.ds(start, NA_KH * W), :]
        s_w = _nt(qs, kw) * scale + bias_ref[r - rs]
        m = jnp.maximum(jnp.max(s_w, axis=-1, keepdims=True), jnp.max(s_c, axis=-1, keepdims=True))
        p_w = jnp.exp(s_w - m)
        p_c = jnp.exp(s_c - m)
        l = jnp.sum(p_w, axis=-1, keepdims=True) + jnp.sum(p_c, axis=-1, keepdims=True)
        write((_dot(p_w.astype(BF16), vw) + _dot(p_c.astype(BF16), vc)) / l)


def _na_call(zc, bias_d, bsz, tb):
    t = zc.shape[0]
    steps = tb // GRID_W
    n_rows = (tb - TM) // GRID_W
    kern = functools.partial(_na_kernel, n_rows=n_rows, ctx_len=TM)
    return pl.pallas_call(
        kern,
        grid=(bsz, steps),
        in_specs=[pl.BlockSpec((GRID_W, C_W), lambda b, i: (b * steps + i, 0)),
                  pl.BlockSpec((tb, C_W), lambda b, i: (b, 1)),
                  pl.BlockSpec((tb, C_W), lambda b, i: (b, 2)),
                  pl.BlockSpec(bias_d.shape, lambda b, i: (0, 0, 0))],
        out_specs=pl.BlockSpec((GRID_W, C_W), lambda b, i: (b * steps + i, 0)),
        out_shape=jax.ShapeDtypeStruct((t, C_W), BF16),
        compiler_params=_cparams(("arbitrary", "arbitrary")),
        name="nbr_attn",
    )(zc, zc, zc, bias_d)


def _merge_kernel(x_ref, mod_ref, oa_ref, ob_ref, oc_ref, wg_ref, bg_ref, wa_ref, wb_ref, wc_ref,
                  wo_ref, lg_ref, lb_ref, wr_ref, br_ref, x1_ref, h2_ref, rt_ref, *, alpha):
    d = x_ref.shape[1]
    x = x_ref[...]
    h = _modulate(x, mod_ref[0, 0:1, :], mod_ref[0, 1:2, :]).astype(BF16)
    y = jnp.zeros_like(x)
    for bi, (o_ref, w_ref) in enumerate(((oa_ref, wa_ref), (ob_ref, wb_ref), (oc_ref, wc_ref))):
        gate = jax.nn.sigmoid(_dot(h, wg_ref[:, bi * d:(bi + 1) * d]) + bg_ref[:, bi * d:(bi + 1) * d])
        y = y + gate * _dot(o_ref[...], w_ref[...])
    y2 = _dot(y.astype(BF16), wo_ref[...])
    x1 = _ln0(alpha * x + mod_ref[0, 2:3, :] * y2) * lg_ref[...] + lb_ref[...]
    x1_ref[...] = x1
    h2 = _modulate(x1, mod_ref[0, 3:4, :], mod_ref[0, 4:5, :])
    h2_ref[...] = h2

    logits = _dot(h2, wr_ref[...], HIGHEST) + br_ref[...]
    lane = lax.broadcasted_iota(jnp.int32, logits.shape, 1)
    big = jnp.int32(1 << 20)
    neg = jnp.float32(-jnp.inf)

    def amax(vals):
        m = jnp.max(vals, axis=-1, keepdims=True)
        return m, jnp.min(jnp.where(vals == m, lane, big), axis=-1, keepdims=True)

    gl = jnp.where(lane < N_GROUPS, logits, neg)
    gmax, grp = amax(gl)
    p_grp = 1.0 / jnp.sum(jnp.exp(gl - gmax), axis=-1, keepdims=True)
    lo = N_GROUPS + grp * EXPERTS_PER_GROUP
    sl = jnp.where((lane >= lo) & (lane < lo + EXPERTS_PER_GROUP), logits, neg)
    v1, i1 = amax(sl)
    v2, i2 = amax(jnp.where(lane == i1, neg, sl))
    e2 = jnp.exp(v2 - v1)
    g1 = p_grp / (1.0 + e2)
    g2 = p_grp * e2 / (1.0 + e2)
    rt = jnp.where(lane == 0, (i1 - N_GROUPS).astype(F32),
                   jnp.where(lane == 1, (i2 - N_GROUPS).astype(F32),
                             jnp.where(lane == 2, g1, jnp.where(lane == 3, g2, 0.0))))
    rt_ref[...] = rt


def _merge_call(xs, mods, oa, ob, oc, wgate, bgate, wa, wb, wc, wo, lg, lb, wr, br, nt_b, alpha):
    t, d = xs.shape
    full = lambda shape: pl.BlockSpec(shape, lambda j: tuple(0 for _ in shape))
    row = lambda w: pl.BlockSpec((TM, w), lambda j: (j, 0))
    return pl.pallas_call(
        functools.partial(_merge_kernel, alpha=alpha),
        grid=(t // TM,),
        in_specs=[row(d), pl.BlockSpec((1, 6, d), _mod_index(nt_b)), row(A_Q), row(B_V), row(C_W),
                  full(wgate.shape), full(bgate.shape), full(wa.shape), full(wb.shape), full(wc.shape),
                  full(wo.shape), full(lg.shape), full(lb.shape), full(wr.shape), full(br.shape)],
        out_specs=[row(d), row(d), row(LANES)],
        out_shape=[jax.ShapeDtypeStruct((t, d), F32), jax.ShapeDtypeStruct((t, d), F32),
                   jax.ShapeDtypeStruct((t, LANES), F32)],
        compiler_params=_cparams(("arbitrary",)),
        name="merge_router",
    )(xs, mods, oa, ob, oc, wgate, bgate, wa, wb, wc, wo, lg, lb, wr, br)


def _dispatch_kernel(pos_ref, h_ref, xs_in_ref, xs_ref, sem):
    del xs_in_ref

    def row_copy(tok, slot):
        return pltpu.make_async_copy(h_ref.at[pl.ds(tok, 1)], xs_ref.at[pl.ds(slot, 1)], sem)

    def issue(tok, carry):
        for kk in range(TOP_K):
            row_copy(tok, pos_ref[0, 0, TOP_K * tok + kk]).start()
        return carry

    lax.fori_loop(0, TM, issue, 0, unroll=DMA_UNROLL)
    for kk in range(TOP_K):
        pltpu.make_async_copy(h_ref, xs_ref.at[pl.ds(0, TM)], sem).wait()


def _dispatch_call(pos3, h2, n_slots):
    t, d = h2.shape
    zeros = jnp.zeros((n_slots, d), F32)
    return pl.pallas_call(
        _dispatch_kernel,
        grid=(t // TM,),
        in_specs=[pl.BlockSpec((1, 1, TOP_K * TM), lambda j: (j, 0, 0), memory_space=pltpu.SMEM),
                  pl.BlockSpec((TM, d), lambda j: (j, 0)),
                  pl.BlockSpec(memory_space=pl.ANY)],
        out_specs=pl.BlockSpec(memory_space=pl.ANY),
        out_shape=jax.ShapeDtypeStruct((n_slots, d), F32),
        scratch_shapes=[pltpu.SemaphoreType.DMA(())],
        input_output_aliases={2: 0},
        compiler_params=_cparams(("arbitrary",)),
        name="moe_dispatch",
    )(pos3, h2, zeros)


def _expert_kernel(be_ref, x_ref, w1_ref, w3_ref, w2_ref, y_ref):
    del be_ref
    xb = x_ref[...].astype(BF16)
    a = _dot(xb, w1_ref[0].astype(BF16))
    b = _dot(xb, w3_ref[0].astype(BF16))
    mid = (a * jax.nn.sigmoid(a) * b).astype(BF16)
    y_ref[...] = _dot(mid, w2_ref[0].astype(BF16))


def _expert_call(block_expert, x_slots, w1, w3, w2):
    n_slots, d = x_slots.shape
    de = w1.shape[2]
    grid_spec = pltpu.PrefetchScalarGridSpec(
        num_scalar_prefetch=1,
        grid=(n_slots // MOE_BLK,),
        in_specs=[pl.BlockSpec((MOE_BLK, d), lambda j, be: (j, 0)),
                  pl.BlockSpec((1, d, de), lambda j, be: (be[j], 0, 0)),
                  pl.BlockSpec((1, d, de), lambda j, be: (be[j], 0, 0)),
                  pl.BlockSpec((1, de, d), lambda j, be: (be[j], 0, 0))],
        out_specs=pl.BlockSpec((MOE_BLK, d), lambda j, be: (j, 0)),
    )
    return pl.pallas_call(
        _expert_kernel,
        grid_spec=grid_spec,
        out_shape=jax.ShapeDtypeStruct((n_slots, d), F32),
        compiler_params=_cparams(("arbitrary",)),
        name="moe_experts",
    )(block_expert, x_slots, w1, w3, w2)


def _combine_kernel(pos_ref, x1_ref, mod_ref, rt_ref, ys_ref, lg_ref, lb_ref, o_ref, buf, sem, *, alpha):
    def row_copy(slot, kk, tok):
        return pltpu.make_async_copy(ys_ref.at[pl.ds(slot, 1)], buf.at[kk, pl.ds(tok, 1)], sem)

    def issue(tok, carry):
        for kk in range(TOP_K):
            row_copy(pos_ref[0, 0, TOP_K * tok + kk], kk, tok).start()
        return carry

    lax.fori_loop(0, TM, issue, 0, unroll=DMA_UNROLL)
    for kk in range(TOP_K):
        pltpu.make_async_copy(ys_ref.at[pl.ds(0, TM)], buf.at[kk], sem).wait()
    rt = rt_ref[...]
    y = rt[:, 2:3] * buf[0] + rt[:, 3:4] * buf[1]
    x2 = _ln0(alpha * x1_ref[...] + mod_ref[0, 5:6, :] * y) * lg_ref[...] + lb_ref[...]
    o_ref[...] = x2


def _combine_call(pos3, x1, mods, rt, y_slots, lg, lb, nt_b, alpha, latent_only):
    t, d = x1.shape
    vec = pl.BlockSpec((1, d), lambda j: (0, 0))
    if latent_only:
        n_lat = nt_b - 1
        n_out = (t // TM // nt_b) * n_lat
        tile = lambda j: (j // n_lat) * nt_b + 1 + j % n_lat
        mod_idx = lambda j: ((j // n_lat) * 2 + 1, 0, 0)
    else:
        n_out = t // TM
        tile = lambda j: j
        mod_idx = _mod_index(nt_b)
    return pl.pallas_call(
        functools.partial(_combine_kernel, alpha=alpha),
        grid=(n_out,),
        in_specs=[pl.BlockSpec((1, 1, TOP_K * TM), lambda j: (tile(j), 0, 0), memory_space=pltpu.SMEM),
                  pl.BlockSpec((TM, d), lambda j: (tile(j), 0)),
                  pl.BlockSpec((1, 6, d), mod_idx),
                  pl.BlockSpec((TM, LANES), lambda j: (tile(j), 0)),
                  pl.BlockSpec(memory_space=pl.ANY), vec, vec],
        out_specs=pl.BlockSpec((TM, d), lambda j: (j, 0)),
        out_shape=jax.ShapeDtypeStruct((n_out * TM, d), F32),
        scratch_shapes=[pltpu.VMEM((TOP_K, TM, d), F32), pltpu.SemaphoreType.DMA(())],
        compiler_params=_cparams(("arbitrary",)),
        name="moe_combine",
    )(pos3, x1, mods, rt, y_slots, lg, lb)


def _slot_plan(rt, n_blocks):
    flat_e = rt[:, 0:TOP_K].astype(jnp.int32).reshape(-1)
    onehot = (flat_e[:, None] == jnp.arange(N_EXPERTS, dtype=jnp.int32)[None, :]).astype(jnp.int32)
    csum = jnp.cumsum(onehot, axis=0)
    counts = csum[-1]
    padded = ((counts + MOE_BLK - 1) // MOE_BLK) * MOE_BLK
    pad_end = jnp.cumsum(padded)
    pad_start = pad_end - padded
    pos = jnp.sum(onehot * (csum - 1 + pad_start[None, :]), axis=1)
    block_start = jnp.arange(n_blocks, dtype=jnp.int32) * MOE_BLK
    block_expert = jnp.minimum(
        jnp.sum((pad_end[None, :] <= block_start[:, None]).astype(jnp.int32), axis=1), N_EXPERTS - 1)
    return pos.astype(jnp.int32), block_expert.astype(jnp.int32)


def _rope_tables(seq_len, ctx_len):
    n_freq = HEAD_DIM // 4
    t = jnp.arange(seq_len)
    invf = ROPE_THETA ** (-jnp.arange(n_freq, dtype=F32) / n_freq)
    ar = (t // GRID_W).astype(F32)[:, None] * invf
    ac = (t % GRID_W).astype(F32)[:, None] * invf
    ang = jnp.concatenate([ar, ar, ac, ac], axis=-1)
    cos, sin = jnp.cos(ang), jnp.sin(ang)
    quarter = (np.arange(HEAD_DIM) // n_freq) % 2
    sa = sin * jnp.asarray(quarter == 1, F32)
    sb = -sin * jnp.asarray(quarter == 0, F32)
    ones = jnp.ones((ctx_len, HEAD_DIM), F32)
    zeros = jnp.zeros((ctx_len, HEAD_DIM), F32)
    tabs = []
    for lat, ctx in ((cos, ones), (sa, zeros), (sb, zeros)):
        tab = jnp.concatenate([ctx, lat], axis=0)
        tabs.append(jnp.tile(tab, (1, LANES // HEAD_DIM)))
    return tabs


def _na_bias_table(rpb, n_rows):
    kh = NA_KH
    w = np.arange(GRID_W)
    col_start = np.clip(w - NA_KW // 2, 0, GRID_W - NA_KW)
    colk = np.arange(GRID_W)
    inside = (colk[None, :] >= col_start[:, None]) & (colk[None, :] < col_start[:, None] + NA_KW)
    col_rel = np.clip(colk[None, :] - w[:, None] + (NA_KW - 1), 0, 2 * NA_KW - 2)
    pats = np.arange(kh)
    row_rel = np.arange(kh)[None, :] - pats[:, None] + (NA_KH - 1)
    b = rpb[:, row_rel][:, :, :, col_rel]
    b = jnp.where(jnp.asarray(inside)[None, None, None], b, NEG_BIG)
    b = b.transpose(1, 0, 3, 2, 4)
    return b.reshape(kh, NA_HEADS * GRID_W, kh * GRID_W).astype(F32)


def _pack_in_proj(w_in, b_in):
    a0 = 0
    a1 = a0 + NCOL_A
    bq0 = a1
    b_end = bq0 + 2 * B_QK + 2 * B_V + B_A
    c_end = b_end + NCOL_C
    d = w_in.shape[0]
    padw = jnp.zeros((d, LANES - B_A), w_in.dtype)
    padb = jnp.zeros((LANES - B_A,), b_in.dtype)
    w_p = jnp.concatenate([w_in[:, a0:a1], w_in[:, bq0:b_end], padw, w_in[:, b_end:c_end]], axis=1)
    b_p = jnp.concatenate([b_in[a0:a1], b_in[bq0:b_end], padb, b_in[b_end:c_end]])
    return w_p.astype(BF16), b_p.reshape(1, -1), w_in[:, c_end:].astype(BF16), b_in[c_end:].reshape(1, -1)


def kernel(x, c, ctx, c_ctx, w_mod, b_mod, w_in, b_in, attn_q_norm, attn_k_norm, gla_w_gate, gla_b_gate,
           gla_norm, na_rpb, w_br_attn, w_br_gla, w_br_na, w_out, ln1_g, ln1_b, w_router_group,
           b_router_group, w_router_expert, b_router_expert, moe_w1, moe_w3, moe_w2, ln2_g, ln2_b):
    bsz, seq, d = x.shape
    ctx_len = ctx.shape[1]
    depth = w_mod.shape[0]
    assert ctx_len == TM and seq % TM == 0 and (seq // GRID_W) >= NA_KH
    tb = ctx_len + seq
    nt_b = tb // TM
    t = bsz * tb
    alpha = (2 * depth) ** 0.25

    xs = jnp.concatenate([ctx, x], axis=1).reshape(t, d)

    cin = jnp.zeros((8, d), F32).at[0:bsz].set(c).at[bsz].set(c_ctx)
    mod_all = _mod_call(cin, w_mod, b_mod)

    tabs = _rope_tables(seq, ctx_len)
    gm = jnp.asarray(np.kron(np.eye(LANES // HEAD_DIM), np.ones((HEAD_DIM, HEAD_DIM))) / HEAD_DIM, BF16)
    n_assign = t * TOP_K
    n_blocks = -(-n_assign // MOE_BLK) + N_EXPERTS
    n_slots = n_blocks * MOE_BLK
    w1_all = moe_w1.reshape((depth * N_EXPERTS,) + moe_w1.shape[2:])
    w3_all = moe_w3.reshape((depth * N_EXPERTS,) + moe_w3.shape[2:])
    w2_all = moe_w2.reshape((depth * N_EXPERTS,) + moe_w2.shape[2:])

    for l in range(depth):
        with_ctx = True
        m = mod_all[l].reshape(8, 6, d)
        mods = jnp.stack([jnp.broadcast_to(m[bsz], (bsz, 6, d)), m[0:bsz]], axis=1).reshape(bsz * 2, 6, d)

        w_p, b_p, w_gate_cols, b_gate_cols = _pack_in_proj(w_in[l], b_in[l])
        za, zb, zc = _proj_call(xs, mods, w_p, b_p, nt_b)

        qn = jnp.tile(attn_q_norm[l], LANES // HEAD_DIM).reshape(1, LANES)
        kn = jnp.tile(attn_k_norm[l], LANES // HEAD_DIM).reshape(1, LANES)
        oa = _gqa_call(za, tabs, qn, kn, gm, bsz, tb, with_ctx)

        o_f = None
        for direction in range(2):
            wg = jnp.zeros((LANES, B_QK), F32).at[direction * GLA_GATE_RANK:(direction + 1) * GLA_GATE_RANK].set(
                gla_w_gate[l, direction])
            bg = gla_b_gate[l, direction].reshape(1, B_QK)
            o_f = _gla_call(zb, o_f, wg, bg, gla_norm[l].reshape(1, GLA_DV), bsz, tb, reverse=direction == 1)
        ob = o_f

        oc = _na_call(zc, _na_bias_table(na_rpb[l], seq // GRID_W), bsz, tb)

        wr = jnp.concatenate([w_router_group[l], w_router_expert[l],
                              jnp.zeros((d, LANES - N_GROUPS - N_EXPERTS), F32)], axis=1)
        br = jnp.concatenate([b_router_group[l], b_router_expert[l],
                              jnp.zeros((LANES - N_GROUPS - N_EXPERTS,), F32)]).reshape(1, LANES)
        x1, h2, rt = _merge_call(xs, mods, oa, ob, oc, w_gate_cols, b_gate_cols,
                                 w_br_attn[l].astype(BF16), w_br_gla[l].astype(BF16), w_br_na[l].astype(BF16),
                                 w_out[l].astype(BF16), ln1_g[l].reshape(1, d), ln1_b[l].reshape(1, d),
                                 wr, br, nt_b, alpha)

        pos, block_expert = _slot_plan(rt, n_blocks)
        pos3 = pos.reshape(t // TM, 1, TOP_K * TM)
        x_slots = _dispatch_call(pos3, h2, n_slots)
        y_slots = _expert_call(block_expert + l * N_EXPERTS, x_slots, w1_all, w3_all, w2_all)
        xs = _combine_call(pos3, x1, mods, rt, y_slots, ln2_g[l].reshape(1, d), ln2_b[l].reshape(1, d),
                           nt_b, alpha, latent_only=l == depth - 1)

    return xs.reshape(bsz, seq, d)
```

```python
import functools

import jax
import jax.numpy as jnp
import numpy as np
from jax import lax
from jax.experimental import pallas as pl
from jax.experimental.pallas import tpu as pltpu

F32 = jnp.float32
BF16 = jnp.bfloat16
HIGHEST = lax.Precision.HIGHEST

GRID_W = 64
HEAD_DIM = 64
ROPE_THETA = 10000.0
ATTN_HEADS = 8
ATTN_KV_HEADS = 2
GLA_HEADS = 4
GLA_DK = 64
GLA_DV = 128
GLA_GATE_RANK = 16
GLA_TAU = 16.0
GLA_CHUNK = 64
GLA_SUB = 8
GLA_LEVELS = 3
NA_HEADS = 8
NA_KH = 8
NA_KW = 16
N_GROUPS = 4
EXPERTS_PER_GROUP = 8
N_EXPERTS = N_GROUPS * EXPERTS_PER_GROUP
TOP_K = 2
LN_EPS = 1e-6
RMS_EPS = 1e-6
NEG_BIG = -1e30
LOG2E = 1.4426950408889634

LANES = 128
TM = 256
MOE_BLK = 256
DMA_UNROLL = 8
VMEM_LIMIT = 56 * 1024 * 1024

A_Q, A_KV = ATTN_HEADS * HEAD_DIM, ATTN_KV_HEADS * HEAD_DIM
B_QK, B_V, B_A = GLA_HEADS * GLA_DK, GLA_HEADS * GLA_DV, 2 * GLA_GATE_RANK
C_W = NA_HEADS * HEAD_DIM
NCOL_A = A_Q + 2 * A_KV
ZA_COLS = A_Q + 3 * ATTN_KV_HEADS * LANES
NCOL_B = 2 * B_QK + 2 * B_V + LANES
NCOL_C = 3 * C_W


def _cparams(sem):
    return pltpu.CompilerParams(dimension_semantics=sem, vmem_limit_bytes=VMEM_LIMIT)


def _ln0(x):
    mu = jnp.mean(x, axis=-1, keepdims=True)
    xc = x - mu
    var = jnp.mean(xc * xc, axis=-1, keepdims=True)
    return xc * lax.rsqrt(var + LN_EPS)


def _modulate(x, shift, scale):
    return _ln0(x) * (1.0 + scale) + shift


def _nt(a, b):
    return lax.dot_general(a, b, (((1,), (1,)), ((), ())), preferred_element_type=F32)


def _dot(a, b, precision=None):
    return jnp.dot(a, b, preferred_element_type=F32, precision=precision)


def _mod_kernel(c_ref, w_ref, b_ref, o_ref):
    c = c_ref[...]
    s = c * jax.nn.sigmoid(c)
    o_ref[0] = _dot(s, w_ref[0]) + b_ref[0]


def _mod_call(cin, w_mod, b_mod):
    depth, d, nmod = w_mod.shape
    nb = nmod // d
    return pl.pallas_call(
        _mod_kernel,
        grid=(depth, nb),
        in_specs=[pl.BlockSpec((8, d), lambda l, n: (0, 0)),
                  pl.BlockSpec((1, d, d), lambda l, n: (l, 0, n)),
                  pl.BlockSpec((1, 1, d), lambda l, n: (l, 0, n))],
        out_specs=pl.BlockSpec((1, 8, d), lambda l, n: (l, 0, n)),
        out_shape=jax.ShapeDtypeStruct((depth, 8, nmod), F32),
        compiler_params=_cparams(("arbitrary", "arbitrary")),
        name="adaln_mod",
    )(cin, w_mod, b_mod.reshape(depth, 1, nmod))


def _proj_kernel(x_ref, mod_ref, w_ref, b_ref, cos_ref, sa_ref, sb_ref, qn_ref, kn_ref, gm_ref,
                 za_ref, zb_ref, zc_ref):
    h = _modulate(x_ref[...], mod_ref[0, 0:1, :], mod_ref[0, 1:2, :]).astype(BF16)
    lo = NCOL_A
    for o_ref in (zb_ref, zc_ref):
        hi = lo + o_ref.shape[1]
        acc = _dot(h, w_ref[:, lo:hi]) + b_ref[:, lo:hi]
        o_ref[...] = acc.astype(o_ref.dtype)
        lo = hi

    gm = gm_ref[...]
    lo_half = lax.broadcasted_iota(jnp.int32, (1, LANES), 1) < HEAD_DIM

    def rms(x, g):
        x2 = x * x
        hi2 = x2.astype(BF16)
        lo2 = (x2 - hi2.astype(F32)).astype(BF16)
        ms = _dot(hi2, gm) + _dot(lo2, gm)
        return x * lax.rsqrt(ms + RMS_EPS) * g

    def rope(x):
        q4 = HEAD_DIM // 4
        return x * cos_ref[...] + pltpu.roll(x, q4, 1) * sa_ref[...] + pltpu.roll(x, LANES - q4, 1) * sb_ref[...]

    def both_halves(x):
        sw = pltpu.roll(x, HEAD_DIM, 1)
        return [jnp.where(lo_half, x, sw), jnp.where(lo_half, sw, x)]

    acc = _dot(h, w_ref[:, 0:NCOL_A]) + b_ref[:, 0:NCOL_A]
    parts = []
    for i in range(A_Q // LANES):
        parts.append(rope(rms(acc[:, i * LANES:(i + 1) * LANES], qn_ref[...])) * (HEAD_DIM ** -0.5 * LOG2E))
    parts += both_halves(rope(rms(acc[:, A_Q:A_Q + A_KV], kn_ref[...])))
    ones = jnp.ones((acc.shape[0], LANES), F32)
    for vv in both_halves(acc[:, A_Q + A_KV:A_Q + 2 * A_KV]):
        parts += [vv, ones]
    za_ref[...] = jnp.concatenate(parts, axis=1).astype(za_ref.dtype)


def _mod_index(nt_b):
    return lambda j: ((j // nt_b) * 2 + jnp.minimum(j % nt_b, 1), 0, 0)


def _proj_call(xs, mods, w_p, b_p, tabs, q_norm, k_norm, gm, nt_b):
    t, d = xs.shape
    ncol = w_p.shape[1]
    tab = pl.BlockSpec((TM, LANES), lambda j: (j % nt_b, 0))
    vec = pl.BlockSpec((1, LANES), lambda j: (0, 0))
    return pl.pallas_call(
        _proj_kernel,
        grid=(t // TM,),
        in_specs=[pl.BlockSpec((TM, d), lambda j: (j, 0)),
                  pl.BlockSpec((1, 6, d), _mod_index(nt_b)),
                  pl.BlockSpec((d, ncol), lambda j: (0, 0)),
                  pl.BlockSpec((1, ncol), lambda j: (0, 0)),
                  tab, tab, tab, vec, vec,
                  pl.BlockSpec((LANES, LANES), lambda j: (0, 0))],
        out_specs=[pl.BlockSpec((TM, ZA_COLS), lambda j: (j, 0)),
                   pl.BlockSpec((TM, NCOL_B), lambda j: (j, 0)),
                   pl.BlockSpec((TM, NCOL_C), lambda j: (j, 0))],
        out_shape=[jax.ShapeDtypeStruct((t, ZA_COLS), BF16),
                   jax.ShapeDtypeStruct((t, NCOL_B), F32),
                   jax.ShapeDtypeStruct((t, NCOL_C), BF16)],
        compiler_params=_cparams(("arbitrary",)),
        name="in_proj",
    )(xs, mods, w_p, b_p, *tabs, q_norm, k_norm, gm)


def _gqa_kernel(q_ref, k_ref, v_ref, o_ref, *, n_keys, ctx_len):
    qi = pl.program_id(2)
    lo_half = lax.broadcasted_iota(jnp.int32, (1, LANES), 1) < HEAD_DIM
    group = ATTN_HEADS // ATTN_KV_HEADS
    zero = jnp.zeros((TM, LANES), BF16)

    def attend(nk):
        outs = []
        for g in range(group):
            xq = q_ref[:, (g // 2) * LANES:(g // 2 + 1) * LANES]
            qm = jnp.where(lo_half if g % 2 == 0 else jnp.logical_not(lo_half), xq, zero)
            s = _nt(qm, k_ref[0:nk, :])
            p = jnp.exp2(s - jnp.max(s, axis=-1, keepdims=True)).astype(BF16)
            ol = _dot(p, v_ref[0:nk, :])
            outs.append(ol[:, 0:LANES] / ol[:, LANES:2 * LANES])
        tiles = [jnp.where(lo_half, outs[2 * i], outs[2 * i + 1]) for i in range(group // 2)]
        o_ref[...] = jnp.concatenate(tiles, axis=1).astype(o_ref.dtype)

    pl.when(qi == 0)(lambda: attend(ctx_len))
    pl.when(qi > 0)(lambda: attend(n_keys))


def _gqa_call(za, bsz, tb):
    t = za.shape[0]
    nt_b = tb // TM
    k_blk = A_Q // LANES
    return pl.pallas_call(
        functools.partial(_gqa_kernel, n_keys=tb, ctx_len=TM),
        grid=(bsz, ATTN_KV_HEADS, nt_b),
        in_specs=[pl.BlockSpec((TM, 2 * LANES), lambda b, h, i: (b * nt_b + i, h)),
                  pl.BlockSpec((tb, LANES), lambda b, h, i: (b, k_blk + h)),
                  pl.BlockSpec((tb, 2 * LANES), lambda b, h, i: (b, (k_blk + ATTN_KV_HEADS) // 2 + h))],
        out_specs=pl.BlockSpec((TM, 2 * LANES), lambda b, h, i: (b * nt_b + i, h)),
        out_shape=jax.ShapeDtypeStruct((t, A_Q), BF16),
        compiler_params=_cparams(("arbitrary", "arbitrary", "arbitrary")),
        name="gqa_axial",
    )(za, za, za)


def _gla_kernel(*refs, reverse):
    if reverse:
        (q_ref, k_ref, v_ref, a_ref, r_ref, of_ref, wg_ref, bg_ref, mat_ref, hsel_ref, lvl_ref, vmask_ref,
         ng_ref, o_ref, st_s) = refs
    else:
        q_ref, k_ref, v_ref, a_ref, wg_ref, bg_ref, mat_ref, hsel_ref, lvl_ref, vmask_ref, o_ref, st_s = refs
    C, SUB, H = GLA_CHUNK, GLA_SUB, GLA_HEADS
    rows_t = q_ref.shape[0]

    @pl.when(pl.program_id(1) == 0)
    def _zero_state():
        st_s[...] = jnp.zeros_like(st_s)

    lane_k = lax.broadcasted_iota(jnp.int32, (1, B_QK), 1) // GLA_DK
    hmask = [lane_k == h for h in range(H)]
    lane_s = lax.broadcasted_iota(jnp.int32, (1, LANES), 1) // SUB

    def split3(x):
        x1 = x.astype(BF16)
        r1 = x - x1.astype(F32)
        x2 = r1.astype(BF16)
        return [x1, x2, (r1 - x2.astype(F32)).astype(BF16)]

    def stack_heads(x):
        return jnp.concatenate([jnp.where(hmask[h], x, 0.0) for h in range(H)], axis=0).astype(BF16)

    def head_diag(y, r, w):
        return jnp.concatenate([y[h * r:(h + 1) * r, h * w:(h + 1) * w] for h in range(H)], axis=1)

    q = q_ref[...] * (GLA_DK ** -0.5)
    k = k_ref[...]
    vb = v_ref[...].astype(BF16)
    a = a_ref[...]
    a_hi = a.astype(BF16)
    a_lo = (a - a_hi.astype(F32)).astype(BF16)
    xw = _dot(a_hi, wg_ref[...])
    x = xw[:, 0:B_QK] + xw[:, B_QK:2 * B_QK] + _dot(a_lo, wg_ref[:, 0:B_QK]) + bg_ref[...]
    g = (jnp.minimum(x, 0.0) - jnp.log(1.0 + jnp.exp(-jnp.abs(x)))) * (1.0 / GLA_TAU)
    cm = _dot(mat_ref[...], jnp.concatenate(split3(g), axis=1))
    b = cm[:, 0:B_QK] + cm[:, B_QK:2 * B_QK] + cm[:, 2 * B_QK:3 * B_QK]

    def group_rows(grp, row_in_group):
        return jnp.concatenate([jnp.broadcast_to(b[r0 + row_in_group:r0 + row_in_group + 1, :], (grp, B_QK))
                                for r0 in range(0, rows_t, grp)], axis=0)

    to_end = group_rows(C, 0 if reverse else C - 1) - b
    qb = q * jnp.exp(b)
    kend = (k * jnp.exp(to_end)).astype(BF16)
    q_lv, k_lv = [], []
    for lv in range(GLA_LEVELS):
        grp = C >> lv
        ref = group_rows(grp, grp // 2 if reverse else grp // 2 - 1)
        e_lv = jnp.exp(-jnp.abs(b - ref))
        q_lv.append(q * e_lv)
        k_lv.append((k * e_lv).astype(BF16))
    terms = []
    for d in range(SUB):
        sh = (rows_t - d) % rows_t if reverse else d
        kd = pltpu.roll(k, sh, 0) if sh else k
        bd = pltpu.roll(b, sh, 0) if sh else b
        terms.append((q * kd * jnp.exp(jnp.minimum(b - bd, 0.0))).astype(BF16))
    sacc = jnp.where(vmask_ref[...] != 0.0, _dot(jnp.concatenate(terms, axis=1), hsel_ref[...]), 0.0)
    level = lvl_ref[...]

    st = st_s[...]
    n_chunks = rows_t // C
    order = range(n_chunks - 1, -1, -1) if reverse else range(n_chunks)
    for c in order:
        rows = slice(c * C, (c + 1) * C)
        diag = []
        for h in range(H):
            xh = jnp.where(lane_s == h, sacc[rows, :], 0.0)
            sh = (LANES - h * SUB - (0 if reverse else SUB - 1)) % LANES
            diag.append(pltpu.roll(xh, sh, 1, stride=1, stride_axis=0))
        p = jnp.concatenate(diag, axis=0)[:, 0:C]
        for lv in range(GLA_LEVELS):
            p = jnp.where(level == lv + 1, _nt(stack_heads(q_lv[lv][rows, :]), k_lv[lv][rows, :]), p)
        o = head_diag(_dot(p.astype(BF16), vb[rows, :]), C, GLA_DV)

        o_int = _nt(stack_heads(qb[rows, :]), st.astype(BF16))
        o = o + jnp.concatenate([o_int[h * C:(h + 1) * C, :] for h in range(H)], axis=1)
        upd = lax.dot_general(vb[rows, :], kend[rows, :], (((0,), (0,)), ((), ())),
                              preferred_element_type=F32)
        new = upd[(H - 1) * GLA_DV:H * GLA_DV, :]
        for h in range(H - 2, -1, -1):
            new = jnp.where(hmask[h], upd[h * GLA_DV:(h + 1) * GLA_DV, :], new)
        r_end = c * C if reverse else (c + 1) * C - 1
        st = st * jnp.exp(b[r_end:r_end + 1, :]) + new

        if reverse:
            ot = o + of_ref[rows, :]
            outs = []
            for h in range(H):
                oh = ot[:, h * GLA_DV:(h + 1) * GLA_DV]
                ms = jnp.mean(oh * oh, axis=-1, keepdims=True)
                outs.append(oh * lax.rsqrt(ms + RMS_EPS) * ng_ref[...])
            r = r_ref[rows, :]
            o_ref[rows, :] = (jnp.concatenate(outs, axis=1) * (r * jax.nn.sigmoid(r))).astype(o_ref.dtype)
        else:
            o_ref[rows, :] = o
    st_s[...] = st


def _gla_consts(reverse):
    C, S, H = GLA_CHUNK, GLA_SUB, GLA_HEADS
    assert C == S * 2 ** GLA_LEVELS and H * S <= LANES
    t = np.arange(TM)
    chunk, tt = t // C, t % C
    tau = (C - 1 - tt) if reverse else tt
    same = chunk[:, None] == chunk[None, :]
    incl = (same & (tau[None, :] <= tau[:, None])).astype(np.float32)
    level = np.zeros((C, C), np.int32)
    for lv in range(GLA_LEVELS):
        grp = C >> lv
        later = (tau % grp) >= grp // 2
        tl, ll = tau[:C], later[:C]
        pair = (tl[:, None] // grp == tl[None, :] // grp) & ll[:, None] & ~ll[None, :]
        level[pair] = lv + 1
    hsel = np.zeros((S, B_QK, LANES), np.float32)
    for d in range(S):
        j = d if reverse else S - 1 - d
        for h in range(H):
            hsel[d, h * GLA_DK:(h + 1) * GLA_DK, h * S + j] = 1.0
    lane = np.arange(LANES)
    d_of_lane = (lane % S) if reverse else (S - 1 - lane % S)
    in_blk = ((tt % S)[:, None] + d_of_lane[None, :] <= S - 1) if reverse else ((tt % S)[:, None] >= d_of_lane[None, :])
    vmask = ((lane < H * S)[None, :] & in_blk).astype(np.float32)
    return (jnp.asarray(incl, BF16), jnp.asarray(hsel.reshape(S * B_QK, LANES), BF16),
            jnp.asarray(np.tile(level, (H, 1))), jnp.asarray(vmask))


def _gla_call(zb, o_fwd, wg, bg, ng, bsz, tb, reverse):
    t = zb.shape[0]
    nt_b = tb // TM
    consts = _gla_consts(reverse)
    const_specs = [pl.BlockSpec(c.shape, lambda b, i: (0, 0)) for c in consts]
    if reverse:
        def blk(b, i):
            return b * nt_b + jnp.where(i == 0, 0, nt_b - i)
    else:
        def blk(b, i):
            return b * nt_b + i
    qs = pl.BlockSpec((TM, B_QK), lambda b, i: (blk(b, i), 0))
    ks = pl.BlockSpec((TM, B_QK), lambda b, i: (blk(b, i), 1))
    vs = pl.BlockSpec((TM, B_V), lambda b, i: (blk(b, i), 1))
    rsp = pl.BlockSpec((TM, B_V), lambda b, i: (blk(b, i), 2))
    asp = pl.BlockSpec((TM, LANES), lambda b, i: (blk(b, i), (2 * B_QK + 2 * B_V) // LANES))
    osp = pl.BlockSpec((TM, B_V), lambda b, i: (blk(b, i), 0))
    wsp = pl.BlockSpec((LANES, 2 * B_QK), lambda b, i: (0, 0))
    bsp = pl.BlockSpec((1, B_QK), lambda b, i: (0, 0))
    if reverse:
        in_specs = [qs, ks, vs, asp, rsp, osp, wsp, bsp] + const_specs + [
            pl.BlockSpec((1, GLA_DV), lambda b, i: (0, 0))]
        args = (zb, zb, zb, zb, zb, o_fwd, wg, bg) + consts + (ng,)
        out_dtype = BF16
    else:
        in_specs = [qs, ks, vs, asp, wsp, bsp] + const_specs
        args = (zb, zb, zb, zb, wg, bg) + consts
        out_dtype = F32
    return pl.pallas_call(
        functools.partial(_gla_kernel, reverse=reverse),
        grid=(bsz, nt_b),
        in_specs=in_specs,
        out_specs=osp,
        out_shape=jax.ShapeDtypeStruct((t, B_V), out_dtype),
        scratch_shapes=[pltpu.VMEM((GLA_DV, B_QK), F32)],
        compiler_params=_cparams(("arbitrary", "arbitrary")),
        name="gla_bwd" if reverse else "gla_fwd",
    )(*args)


def _na_kernel(q_ref, k_ref, v_ref, bias_ref, o_ref, *, n_rows, ctx_len):
    i = pl.program_id(1)
    n_ctx_steps = ctx_len // GRID_W
    W = GRID_W
    lo_half = lax.broadcasted_iota(jnp.int32, (1, LANES), 1) < HEAD_DIM
    scale = HEAD_DIM ** -0.5
    n_pairs = C_W // LANES

    def pair_queries(pr):
        qp = q_ref[:, pr * LANES:(pr + 1) * LANES]
        zero = jnp.zeros_like(qp)
        return jnp.concatenate([jnp.where(lo_half, qp, zero), jnp.where(lo_half, zero, qp)], axis=0)

    def pair_out(o):
        return jnp.where(lo_half, o[0:W, :], o[W:2 * W, :])

    @pl.when(i < n_ctx_steps)
    def _ctx_queries():
        outs = []
        for pr in range(n_pairs):
            cols = slice(pr * LANES, (pr + 1) * LANES)
            s_c = _nt(pair_queries(pr), k_ref[0:ctx_len, cols]) * scale
            m = jnp.max(s_c, axis=-1, keepdims=True)
            p = jnp.exp(s_c - m)
            l = jnp.sum(p, axis=-1, keepdims=True)
            outs.append(pair_out(_dot(p.astype(BF16), v_ref[0:ctx_len, cols]) / l))
        o_ref[...] = jnp.concatenate(outs, axis=1).astype(o_ref.dtype)

    @pl.when(i >= n_ctx_steps)
    def _grid_row():
        r = i - n_ctx_steps
        rs = jnp.clip(r - NA_KH // 2, 0, n_rows - NA_KH)
        start = pl.multiple_of(ctx_len + rs * W, W)
        win = pl.ds(start, NA_KH * W)
        outs = []
        for pr in range(n_pairs):
            cols = slice(pr * LANES, (pr + 1) * LANES)
            qs = pair_queries(pr)
            s_c = _nt(qs, k_ref[0:ctx_len, cols]) * scale
            s_w = _nt(qs, k_ref[win, cols]) * scale + bias_ref[r - rs, pr * 2 * W:(pr + 1) * 2 * W, :]
            m = jnp.maximum(jnp.max(s_w, axis=-1, keepdims=True), jnp.max(s_c, axis=-1, keepdims=True))
            p_w = jnp.exp(s_w - m)
            p_c = jnp.exp(s_c - m)
            l = jnp.sum(p_w, axis=-1, keepdims=True) + jnp.sum(p_c, axis=-1, keepdims=True)
            o = _dot(p_w.astype(BF16), v_ref[win, cols]) + _dot(p_c.astype(BF16), v_ref[0:ctx_len, cols])
            outs.append(pair_out(o / l))
        o_ref[...] = jnp.concatenate(outs, axis=1).astype(o_ref.dtype)


def _na_call(zc, bias_d, bsz, tb):
    t = zc.shape[0]
    steps = tb // GRID_W
    n_rows = (tb - TM) // GRID_W
    kern = functools.partial(_na_kernel, n_rows=n_rows, ctx_len=TM)
    return pl.pallas_call(
        kern,
        grid=(bsz, steps),
        in_specs=[pl.BlockSpec((GRID_W, C_W), lambda b, i: (b * steps + i, 0)),
                  pl.BlockSpec((tb, C_W), lambda b, i: (b, 1)),
                  pl.BlockSpec((tb, C_W), lambda b, i: (b, 2)),
                  pl.BlockSpec(bias_d.shape, lambda b, i: (0, 0, 0))],
        out_specs=pl.BlockSpec((GRID_W, C_W), lambda b, i: (b * steps + i, 0)),
        out_shape=jax.ShapeDtypeStruct((t, C_W), BF16),
        compiler_params=_cparams(("arbitrary", "arbitrary")),
        name="nbr_attn",
    )(zc, zc, zc, bias_d)


def _merge_kernel(x_ref, mod_ref, oa_ref, ob_ref, oc_ref, wg_ref, bg_ref, wa_ref, wb_ref, wc_ref,
                  wo_ref, lg_ref, lb_ref, wr_ref, br_ref, x1_ref, h2_ref, rt_ref, *, alpha):
    d = x_ref.shape[1]
    x = x_ref[...]
    h = _modulate(x, mod_ref[0, 0:1, :], mod_ref[0, 1:2, :]).astype(BF16)
    y = jnp.zeros_like(x)
    for bi, (o_ref, w_ref) in enumerate(((oa_ref, wa_ref), (ob_ref, wb_ref), (oc_ref, wc_ref))):
        gate = jax.nn.sigmoid(_dot(h, wg_ref[:, bi * d:(bi + 1) * d]) + bg_ref[:, bi * d:(bi + 1) * d])
        y = y + gate * _dot(o_ref[...], w_ref[...])
    y2 = _dot(y.astype(BF16), wo_ref[...])
    x1 = _ln0(alpha * x + mod_ref[0, 2:3, :] * y2) * lg_ref[...] + lb_ref[...]
    x1_ref[...] = x1
    h2 = _modulate(x1, mod_ref[0, 3:4, :], mod_ref[0, 4:5, :])
    h2_ref[...] = h2

    h2_hi = h2.astype(BF16)
    h2_lo = (h2 - h2_hi.astype(F32)).astype(BF16)
    hw = _dot(h2_hi, wr_ref[...])
    logits = hw[:, 0:LANES] + hw[:, LANES:2 * LANES] + _dot(h2_lo, wr_ref[:, 0:LANES]) + br_ref[...]
    lane = lax.broadcasted_iota(jnp.int32, logits.shape, 1)
    big = jnp.int32(1 << 20)
    neg = jnp.float32(-jnp.inf)

    def amax(vals):
        m = jnp.max(vals, axis=-1, keepdims=True)
        return m, jnp.min(jnp.where(vals == m, lane, big), axis=-1, keepdims=True)

    gl = jnp.where(lane < N_GROUPS, logits, neg)
    gmax, grp = amax(gl)
    p_grp = 1.0 / jnp.sum(jnp.exp(gl - gmax), axis=-1, keepdims=True)
    lo = N_GROUPS + grp * EXPERTS_PER_GROUP
    sl = jnp.where((lane >= lo) & (lane < lo + EXPERTS_PER_GROUP), logits, neg)
    v1, i1 = amax(sl)
    v2, i2 = amax(jnp.where(lane == i1, neg, sl))
    e2 = jnp.exp(v2 - v1)
    g1 = p_grp / (1.0 + e2)
    g2 = p_grp * e2 / (1.0 + e2)
    rt = jnp.where(lane == 0, (i1 - N_GROUPS).astype(F32),
                   jnp.where(lane == 1, (i2 - N_GROUPS).astype(F32),
                             jnp.where(lane == 2, g1, jnp.where(lane == 3, g2, 0.0))))
    rt_ref[...] = rt


def _merge_call(xs, mods, oa, ob, oc, wgate, bgate, wa, wb, wc, wo, lg, lb, wr, br, nt_b, alpha):
    t, d = xs.shape
    full = lambda shape: pl.BlockSpec(shape, lambda j: tuple(0 for _ in shape))
    row = lambda w: pl.BlockSpec((TM, w), lambda j: (j, 0))
    return pl.pallas_call(
        functools.partial(_merge_kernel, alpha=alpha),
        grid=(t // TM,),
        in_specs=[row(d), pl.BlockSpec((1, 6, d), _mod_index(nt_b)), row(A_Q), row(B_V), row(C_W),
                  full(wgate.shape), full(bgate.shape), full(wa.shape), full(wb.shape), full(wc.shape),
                  full(wo.shape), full(lg.shape), full(lb.shape), full(wr.shape), full(br.shape)],
        out_specs=[row(d), row(d), row(LANES)],
        out_shape=[jax.ShapeDtypeStruct((t, d), F32), jax.ShapeDtypeStruct((t, d), F32),
                   jax.ShapeDtypeStruct((t, LANES), F32)],
        compiler_params=_cparams(("arbitrary",)),
        name="merge_router",
    )(xs, mods, oa, ob, oc, wgate, bgate, wa, wb, wc, wo, lg, lb, wr, br)


def _dispatch_kernel(pos_ref, h_ref, xs_in_ref, xs_ref, sem):
    del xs_in_ref

    def row_copy(tok, slot):
        return pltpu.make_async_copy(h_ref.at[pl.ds(tok, 1)], xs_ref.at[pl.ds(slot, 1)], sem)

    def issue(tok, carry):
        for kk in range(TOP_K):
            row_copy(tok, pos_ref[0, 0, TOP_K * tok + kk]).start()
        return carry

    lax.fori_loop(0, TM, issue, 0, unroll=DMA_UNROLL)
    for kk in range(TOP_K):
        pltpu.make_async_copy(h_ref, xs_ref.at[pl.ds(0, TM)], sem).wait()


def _dispatch_call(pos3, h2, n_slots):
    t, d = h2.shape
    zeros = jnp.zeros((n_slots, d), F32)
    return pl.pallas_call(
        _dispatch_kernel,
        grid=(t // TM,),
        in_specs=[pl.BlockSpec((1, 1, TOP_K * TM), lambda j: (j, 0, 0), memory_space=pltpu.SMEM),
                  pl.BlockSpec((TM, d), lambda j: (j, 0)),
                  pl.BlockSpec(memory_space=pl.ANY)],
        out_specs=pl.BlockSpec(memory_space=pl.ANY),
        out_shape=jax.ShapeDtypeStruct((n_slots, d), F32),
        scratch_shapes=[pltpu.SemaphoreType.DMA(())],
        input_output_aliases={2: 0},
        compiler_params=_cparams(("arbitrary",)),
        name="moe_dispatch",
    )(pos3, h2, zeros)


def _expert_kernel(be_ref, x_ref, w1_ref, w3_ref, w2_ref, y_ref):
    del be_ref
    xb = x_ref[...].astype(BF16)
    a = _dot(xb, w1_ref[0].astype(BF16))
    b = _dot(xb, w3_ref[0].astype(BF16))
    mid = (a * jax.nn.sigmoid(a) * b).astype(BF16)
    y_ref[...] = _dot(mid, w2_ref[0].astype(BF16))


def _expert_call(block_expert, x_slots, w1, w3, w2):
    n_slots, d = x_slots.shape
    de = w1.shape[2]
    grid_spec = pltpu.PrefetchScalarGridSpec(
        num_scalar_prefetch=1,
        grid=(n_slots // MOE_BLK,),
        in_specs=[pl.BlockSpec((MOE_BLK, d), lambda j, be: (j, 0)),
                  pl.BlockSpec((1, d, de), lambda j, be: (be[j], 0, 0)),
                  pl.BlockSpec((1, d, de), lambda j, be: (be[j], 0, 0)),
                  pl.BlockSpec((1, de, d), lambda j, be: (be[j], 0, 0))],
        out_specs=pl.BlockSpec((MOE_BLK, d), lambda j, be: (j, 0)),
    )
    return pl.pallas_call(
        _expert_kernel,
        grid_spec=grid_spec,
        out_shape=jax.ShapeDtypeStruct((n_slots, d), F32),
        compiler_params=_cparams(("arbitrary",)),
        name="moe_experts",
    )(block_expert, x_slots, w1, w3, w2)


def _combine_kernel(pos_ref, x1_ref, mod_ref, rt_ref, ys_ref, lg_ref, lb_ref, o_ref, buf, sem, *, alpha):
    def row_copy(slot, kk, tok):
        return pltpu.make_async_copy(ys_ref.at[pl.ds(slot, 1)], buf.at[kk, pl.ds(tok, 1)], sem)

    def issue(tok, carry):
        for kk in range(TOP_K):
            row_copy(pos_ref[0, 0, TOP_K * tok + kk], kk, tok).start()
        return carry

    lax.fori_loop(0, TM, issue, 0, unroll=DMA_UNROLL)
    for kk in range(TOP_K):
        pltpu.make_async_copy(ys_ref.at[pl.ds(0, TM)], buf.at[kk], sem).wait()
    rt = rt_ref[...]
    y = rt[:, 2:3] * buf[0] + rt[:, 3:4] * buf[1]
    x2 = _ln0(alpha * x1_ref[...] + mod_ref[0, 5:6, :] * y) * lg_ref[...] + lb_ref[...]
    o_ref[...] = x2


def _combine_call(pos3, x1, mods, rt, y_slots, lg, lb, nt_b, alpha, latent_only):
    t, d = x1.shape
    vec = pl.BlockSpec((1, d), lambda j: (0, 0))
    if latent_only:
        n_lat = nt_b - 1
        n_out = (t // TM // nt_b) * n_lat
        tile = lambda j: (j // n_lat) * nt_b + 1 + j % n_lat
        mod_idx = lambda j: ((j // n_lat) * 2 + 1, 0, 0)
    else:
        n_out = t // TM
        tile = lambda j: j
        mod_idx = _mod_index(nt_b)
    return pl.pallas_call(
        functools.partial(_combine_kernel, alpha=alpha),
        grid=(n_out,),
        in_specs=[pl.BlockSpec((1, 1, TOP_K * TM), lambda j: (tile(j), 0, 0), memory_space=pltpu.SMEM),
                  pl.BlockSpec((TM, d), lambda j: (tile(j), 0)),
                  pl.BlockSpec((1, 6, d), mod_idx),
                  pl.BlockSpec((TM, LANES), lambda j: (tile(j), 0)),
                  pl.BlockSpec(memory_space=pl.ANY), vec, vec],
        out_specs=pl.BlockSpec((TM, d), lambda j: (j, 0)),
        out_shape=jax.ShapeDtypeStruct((n_out * TM, d), F32),
        scratch_shapes=[pltpu.VMEM((TOP_K, TM, d), F32), pltpu.SemaphoreType.DMA(())],
        compiler_params=_cparams(("arbitrary",)),
        name="moe_combine",
    )(pos3, x1, mods, rt, y_slots, lg, lb)


def _slot_plan(rt, n_blocks):
    flat_e = rt[:, 0:TOP_K].astype(jnp.int32).reshape(-1)
    onehot = (flat_e[:, None] == jnp.arange(N_EXPERTS, dtype=jnp.int32)[None, :]).astype(jnp.int32)
    csum = jnp.cumsum(onehot, axis=0)
    counts = csum[-1]
    padded = ((counts + MOE_BLK - 1) // MOE_BLK) * MOE_BLK
    pad_end = jnp.cumsum(padded)
    pad_start = pad_end - padded
    pos = jnp.sum(onehot * (csum - 1 + pad_start[None, :]), axis=1)
    block_start = jnp.arange(n_blocks, dtype=jnp.int32) * MOE_BLK
    block_expert = jnp.minimum(
        jnp.sum((pad_end[None, :] <= block_start[:, None]).astype(jnp.int32), axis=1), N_EXPERTS - 1)
    return pos.astype(jnp.int32), block_expert.astype(jnp.int32)


def _rope_tables(seq_len, ctx_len):
    n_freq = HEAD_DIM // 4
    t = jnp.arange(seq_len)
    invf = ROPE_THETA ** (-jnp.arange(n_freq, dtype=F32) / n_freq)
    ar = (t // GRID_W).astype(F32)[:, None] * invf
    ac = (t % GRID_W).astype(F32)[:, None] * invf
    ang = jnp.concatenate([ar, ar, ac, ac], axis=-1)
    cos, sin = jnp.cos(ang), jnp.sin(ang)
    quarter = (np.arange(HEAD_DIM) // n_freq) % 2
    sa = sin * jnp.asarray(quarter == 1, F32)
    sb = -sin * jnp.asarray(quarter == 0, F32)
    ones = jnp.ones((ctx_len, HEAD_DIM), F32)
    zeros = jnp.zeros((ctx_len, HEAD_DIM), F32)
    tabs = []
    for lat, ctx in ((cos, ones), (sa, zeros), (sb, zeros)):
        tab = jnp.concatenate([ctx, lat], axis=0)
        tabs.append(jnp.tile(tab, (1, LANES // HEAD_DIM)))
    return tabs


def _na_bias_table(rpb, n_rows):
    kh = NA_KH
    w = np.arange(GRID_W)
    col_start = np.clip(w - NA_KW // 2, 0, GRID_W - NA_KW)
    colk = np.arange(GRID_W)
    inside = (colk[None, :] >= col_start[:, None]) & (colk[None, :] < col_start[:, None] + NA_KW)
    col_rel = np.clip(colk[None, :] - w[:, None] + (NA_KW - 1), 0, 2 * NA_KW - 2)
    pats = np.arange(kh)
    row_rel = np.arange(kh)[None, :] - pats[:, None] + (NA_KH - 1)
    b = rpb[:, row_rel][:, :, :, col_rel]
    b = jnp.where(jnp.asarray(inside)[None, None, None], b, NEG_BIG)
    b = b.transpose(1, 0, 3, 2, 4)
    return b.reshape(kh, NA_HEADS * GRID_W, kh * GRID_W).astype(F32)


def _pack_in_proj(w_in, b_in):
    a0 = 0
    a1 = a0 + NCOL_A
    bq0 = a1
    b_end = bq0 + 2 * B_QK + 2 * B_V + B_A
    c_end = b_end + NCOL_C
    d = w_in.shape[0]
    padw = jnp.zeros((d, LANES - B_A), w_in.dtype)
    padb = jnp.zeros((LANES - B_A,), b_in.dtype)
    w_p = jnp.concatenate([w_in[:, a0:a1], w_in[:, bq0:b_end], padw, w_in[:, b_end:c_end]], axis=1)
    b_p = jnp.concatenate([b_in[a0:a1], b_in[bq0:b_end], padb, b_in[b_end:c_end]])
    return w_p.astype(BF16), b_p.reshape(1, -1), w_in[:, c_end:].astype(BF16), b_in[c_end:].reshape(1, -1)


def kernel(x, c, ctx, c_ctx, w_mod, b_mod, w_in, b_in, attn_q_norm, attn_k_norm, gla_w_gate, gla_b_gate,
           gla_norm, na_rpb, w_br_attn, w_br_gla, w_br_na, w_out, ln1_g, ln1_b, w_router_group,
           b_router_group, w_router_expert, b_router_expert, moe_w1, moe_w3, moe_w2, ln2_g, ln2_b):
    bsz, seq, d = x.shape
    ctx_len = ctx.shape[1]
    depth = w_mod.shape[0]
    assert ctx_len == TM and seq % TM == 0 and (seq // GRID_W) >= NA_KH
    tb = ctx_len + seq
    nt_b = tb // TM
    t = bsz * tb
    alpha = (2 * depth) ** 0.25

    xs = jnp.concatenate([ctx, x], axis=1).reshape(t, d)

    cin = jnp.zeros((8, d), F32).at[0:bsz].set(c).at[bsz].set(c_ctx)
    mod_all = _mod_call(cin, w_mod, b_mod)

    tabs = _rope_tables(seq, ctx_len)
    gm = jnp.asarray(np.kron(np.eye(LANES // HEAD_DIM), np.ones((HEAD_DIM, HEAD_DIM))) / HEAD_DIM, BF16)
    n_assign = t * TOP_K
    n_blocks = -(-n_assign // MOE_BLK) + N_EXPERTS
    n_slots = n_blocks * MOE_BLK
    w1_all = moe_w1.reshape((depth * N_EXPERTS,) + moe_w1.shape[2:])
    w3_all = moe_w3.reshape((depth * N_EXPERTS,) + moe_w3.shape[2:])
    w2_all = moe_w2.reshape((depth * N_EXPERTS,) + moe_w2.shape[2:])

    for l in range(depth):
        m = mod_all[l].reshape(8, 6, d)
        mods = jnp.stack([jnp.broadcast_to(m[bsz], (bsz, 6, d)), m[0:bsz]], axis=1).reshape(bsz * 2, 6, d)

        w_p, b_p, w_gate_cols, b_gate_cols = _pack_in_proj(w_in[l], b_in[l])
        qn = jnp.tile(attn_q_norm[l], LANES // HEAD_DIM).reshape(1, LANES)
        kn = jnp.tile(attn_k_norm[l], LANES // HEAD_DIM).reshape(1, LANES)
        za, zb, zc = _proj_call(xs, mods, w_p, b_p, tabs, qn, kn, gm, nt_b)
        oa = _gqa_call(za, bsz, tb)

        o_f = None
        for direction in range(2):
            wg = jnp.zeros((LANES, B_QK), F32).at[direction * GLA_GATE_RANK:(direction + 1) * GLA_GATE_RANK].set(
                gla_w_gate[l, direction])
            wg_hi = wg.astype(BF16)
            wg = jnp.concatenate([wg_hi, (wg - wg_hi.astype(F32)).astype(BF16)], axis=1)
            bg = gla_b_gate[l, direction].reshape(1, B_QK)
            o_f = _gla_call(zb, o_f, wg, bg, gla_norm[l].reshape(1, GLA_DV), bsz, tb, reverse=direction == 1)
        ob = o_f

        oc = _na_call(zc, _na_bias_table(na_rpb[l], seq // GRID_W), bsz, tb)

        wr = jnp.concatenate([w_router_group[l], w_router_expert[l],
                              jnp.zeros((d, LANES - N_GROUPS - N_EXPERTS), F32)], axis=1)
        br = jnp.concatenate([b_router_group[l], b_router_expert[l],
                              jnp.zeros((LANES - N_GROUPS - N_EXPERTS,), F32)]).reshape(1, LANES)
        wr_hi = wr.astype(BF16)
        wr = jnp.concatenate([wr_hi, (wr - wr_hi.astype(F32)).astype(BF16)], axis=1)
        x1, h2, rt = _merge_call(xs, mods, oa, ob, oc, w_gate_cols, b_gate_cols,
                                 w_br_attn[l].astype(BF16), w_br_gla[l].astype(BF16), w_br_na[l].astype(BF16),
                                 w_out[l].astype(BF16), ln1_g[l].reshape(1, d), ln1_b[l].reshape(1, d),
                                 wr, br, nt_b, alpha)

        pos, block_expert = _slot_plan(rt, n_blocks)
        pos3 = pos.reshape(t // TM, 1, TOP_K * TM)
        x_slots = _dispatch_call(pos3, h2, n_slots)
        y_slots = _expert_call(block_expert + l * N_EXPERTS, x_slots, w1_all, w3_all, w2_all)
        xs = _combine_call(pos3, x1, mods, rt, y_slots, ln2_g[l].reshape(1, d), ln2_b[l].reshape(1, d),
                           nt_b, alpha, latent_only=l == depth - 1)

    return xs.reshape(bsz, seq, d)
```

```python
import functools

import jax
import jax.numpy as jnp
import numpy as np
from jax import lax
from jax.experimental import pallas as pl
from jax.experimental.pallas import tpu as pltpu

F32 = jnp.float32
BF16 = jnp.bfloat16
HIGHEST = lax.Precision.HIGHEST

GRID_W = 64
HEAD_DIM = 64
ROPE_THETA = 10000.0
ATTN_HEADS = 8
ATTN_KV_HEADS = 2
GLA_HEADS = 4
GLA_DK = 64
GLA_DV = 128
GLA_GATE_RANK = 16
GLA_TAU = 16.0
GLA_CHUNK = 64
GLA_SUB = 8
GLA_LEVELS = 3
NA_HEADS = 8
NA_KH = 8
NA_KW = 16
NA_ROWS_PER_STEP = 2
N_GROUPS = 4
EXPERTS_PER_GROUP = 8
N_EXPERTS = N_GROUPS * EXPERTS_PER_GROUP
TOP_K = 2
LN_EPS = 1e-6
RMS_EPS = 1e-6
NEG_BIG = -1e30
LOG2E = 1.4426950408889634

LANES = 128
TM = 256
MOE_BLK = 256
DMA_UNROLL = 8
VMEM_LIMIT = 56 * 1024 * 1024

A_Q, A_KV = ATTN_HEADS * HEAD_DIM, ATTN_KV_HEADS * HEAD_DIM
B_QK, B_V, B_A = GLA_HEADS * GLA_DK, GLA_HEADS * GLA_DV, 2 * GLA_GATE_RANK
C_W = NA_HEADS * HEAD_DIM
NCOL_A = A_Q + 2 * A_KV
ZA_COLS = A_Q + 3 * ATTN_KV_HEADS * LANES
NCOL_B = 2 * B_QK + 2 * B_V + LANES
NCOL_C = 3 * C_W


def _cparams(sem):
    return pltpu.CompilerParams(dimension_semantics=sem, vmem_limit_bytes=VMEM_LIMIT)


def _ln0(x):
    mu = jnp.mean(x, axis=-1, keepdims=True)
    xc = x - mu
    var = jnp.mean(xc * xc, axis=-1, keepdims=True)
    return xc * lax.rsqrt(var + LN_EPS)


def _modulate(x, shift, scale):
    return _ln0(x) * (1.0 + scale) + shift


def _nt(a, b):
    return lax.dot_general(a, b, (((1,), (1,)), ((), ())), preferred_element_type=F32)


def _dot(a, b, precision=None):
    return jnp.dot(a, b, preferred_element_type=F32, precision=precision)


def _mod_kernel(c_ref, w_ref, b_ref, o_ref):
    c = c_ref[...]
    s = c * jax.nn.sigmoid(c)
    o_ref[0] = _dot(s, w_ref[0]) + b_ref[0]


def _mod_call(cin, w_mod, b_mod):
    depth, d, nmod = w_mod.shape
    nb = nmod // d
    return pl.pallas_call(
        _mod_kernel,
        grid=(depth, nb),
        in_specs=[pl.BlockSpec((8, d), lambda l, n: (0, 0)),
                  pl.BlockSpec((1, d, d), lambda l, n: (l, 0, n)),
                  pl.BlockSpec((1, 1, d), lambda l, n: (l, 0, n))],
        out_specs=pl.BlockSpec((1, 8, d), lambda l, n: (l, 0, n)),
        out_shape=jax.ShapeDtypeStruct((depth, 8, nmod), F32),
        compiler_params=_cparams(("arbitrary", "arbitrary")),
        name="adaln_mod",
    )(cin, w_mod, b_mod.reshape(depth, 1, nmod))


def _proj_kernel(x_ref, mod_ref, w_ref, b_ref, cos_ref, sa_ref, sb_ref, qn_ref, kn_ref, gm_ref,
                 za_ref, zb_ref, zc_ref):
    h = _modulate(x_ref[...], mod_ref[0, 0:1, :], mod_ref[0, 1:2, :]).astype(BF16)
    lo = NCOL_A
    for o_ref in (zb_ref, zc_ref):
        hi = lo + o_ref.shape[1]
        acc = _dot(h, w_ref[:, lo:hi]) + b_ref[:, lo:hi]
        o_ref[...] = acc.astype(o_ref.dtype)
        lo = hi

    gm = gm_ref[...]
    lo_half = lax.broadcasted_iota(jnp.int32, (1, LANES), 1) < HEAD_DIM

    def rms(x, g):
        x2 = x * x
        hi2 = x2.astype(BF16)
        lo2 = (x2 - hi2.astype(F32)).astype(BF16)
        ms = _dot(hi2, gm) + _dot(lo2, gm)
        return x * lax.rsqrt(ms + RMS_EPS) * g

    def rope(x):
        q4 = HEAD_DIM // 4
        return x * cos_ref[...] + pltpu.roll(x, q4, 1) * sa_ref[...] + pltpu.roll(x, LANES - q4, 1) * sb_ref[...]

    def both_halves(x):
        sw = pltpu.roll(x, HEAD_DIM, 1)
        return [jnp.where(lo_half, x, sw), jnp.where(lo_half, sw, x)]

    acc = _dot(h, w_ref[:, 0:NCOL_A]) + b_ref[:, 0:NCOL_A]
    parts = []
    for i in range(A_Q // LANES):
        parts.append(rope(rms(acc[:, i * LANES:(i + 1) * LANES], qn_ref[...])) * (HEAD_DIM ** -0.5 * LOG2E))
    parts += both_halves(rope(rms(acc[:, A_Q:A_Q + A_KV], kn_ref[...])))
    ones = jnp.ones((acc.shape[0], LANES), F32)
    for vv in both_halves(acc[:, A_Q + A_KV:A_Q + 2 * A_KV]):
        parts += [vv, ones]
    za_ref[...] = jnp.concatenate(parts, axis=1).astype(za_ref.dtype)


def _mod_index(nt_b, n_tiles):
    ctx_row = n_tiles // nt_b
    return lambda j: (jnp.where(j % nt_b == 0, ctx_row, j // nt_b), 0, 0)


def _proj_call(xs, mods, w_p, b_p, tabs, q_norm, k_norm, gm, nt_b):
    t, d = xs.shape
    ncol = w_p.shape[1]
    tab = pl.BlockSpec((TM, LANES), lambda j: (j % nt_b, 0))
    vec = pl.BlockSpec((1, LANES), lambda j: (0, 0))
    return pl.pallas_call(
        _proj_kernel,
        grid=(t // TM,),
        in_specs=[pl.BlockSpec((TM, d), lambda j: (j, 0)),
                  pl.BlockSpec((1, 6, d), _mod_index(nt_b, t // TM)),
                  pl.BlockSpec((d, ncol), lambda j: (0, 0)),
                  pl.BlockSpec((1, ncol), lambda j: (0, 0)),
                  tab, tab, tab, vec, vec,
                  pl.BlockSpec((LANES, LANES), lambda j: (0, 0))],
        out_specs=[pl.BlockSpec((TM, ZA_COLS), lambda j: (j, 0)),
                   pl.BlockSpec((TM, NCOL_B), lambda j: (j, 0)),
                   pl.BlockSpec((TM, NCOL_C), lambda j: (j, 0))],
        out_shape=[jax.ShapeDtypeStruct((t, ZA_COLS), BF16),
                   jax.ShapeDtypeStruct((t, NCOL_B), F32),
                   jax.ShapeDtypeStruct((t, NCOL_C), BF16)],
        compiler_params=_cparams(("arbitrary",)),
        name="in_proj",
    )(xs, mods, w_p, b_p, *tabs, q_norm, k_norm, gm)


def _gqa_kernel(q_ref, k_ref, v_ref, o_ref, *, n_keys, ctx_len):
    qi = pl.program_id(2)
    lo_half = lax.broadcasted_iota(jnp.int32, (1, LANES), 1) < HEAD_DIM
    group = ATTN_HEADS // ATTN_KV_HEADS
    zero = jnp.zeros((TM, LANES), BF16)

    def attend(nk):
        outs = []
        for g in range(group):
            xq = q_ref[:, (g // 2) * LANES:(g // 2 + 1) * LANES]
            qm = jnp.where(lo_half if g % 2 == 0 else jnp.logical_not(lo_half), xq, zero)
            s = _nt(qm, k_ref[0:nk, :])
            p = jnp.exp2(s - jnp.max(s, axis=-1, keepdims=True)).astype(BF16)
            ol = _dot(p, v_ref[0:nk, :])
            outs.append(ol[:, 0:LANES] / ol[:, LANES:2 * LANES])
        tiles = [jnp.where(lo_half, outs[2 * i], outs[2 * i + 1]) for i in range(group // 2)]
        o_ref[...] = jnp.concatenate(tiles, axis=1).astype(o_ref.dtype)

    pl.when(qi == 0)(lambda: attend(ctx_len))
    pl.when(qi > 0)(lambda: attend(n_keys))


def _gqa_call(za, bsz, tb):
    t = za.shape[0]
    nt_b = tb // TM
    k_blk = A_Q // LANES
    return pl.pallas_call(
        functools.partial(_gqa_kernel, n_keys=tb, ctx_len=TM),
        grid=(bsz, ATTN_KV_HEADS, nt_b),
        in_specs=[pl.BlockSpec((TM, 2 * LANES), lambda b, h, i: (b * nt_b + i, h)),
                  pl.BlockSpec((tb, LANES), lambda b, h, i: (b, k_blk + h)),
                  pl.BlockSpec((tb, 2 * LANES), lambda b, h, i: (b, (k_blk + ATTN_KV_HEADS) // 2 + h))],
        out_specs=pl.BlockSpec((TM, 2 * LANES), lambda b, h, i: (b * nt_b + i, h)),
        out_shape=jax.ShapeDtypeStruct((t, A_Q), BF16),
        compiler_params=_cparams(("arbitrary", "arbitrary", "arbitrary")),
        name="gqa_axial",
    )(za, za, za)


def _gla_kernel(*refs, reverse):
    if reverse:
        (q_ref, k_ref, v_ref, a_ref, r_ref, of_ref, wg_ref, bg_ref, mat_ref, hsel_ref, lvl_ref, vmask_ref,
         ng_ref, o_ref, st_s) = refs
    else:
        q_ref, k_ref, v_ref, a_ref, wg_ref, bg_ref, mat_ref, hsel_ref, lvl_ref, vmask_ref, o_ref, st_s = refs
    C, SUB, H = GLA_CHUNK, GLA_SUB, GLA_HEADS
    rows_t = q_ref.shape[0]

    @pl.when(pl.program_id(1) == 0)
    def _zero_state():
        st_s[...] = jnp.zeros_like(st_s)

    lane_k = lax.broadcasted_iota(jnp.int32, (1, B_QK), 1) // GLA_DK
    hmask = [lane_k == h for h in range(H)]
    lane_s = lax.broadcasted_iota(jnp.int32, (1, LANES), 1) // SUB

    def split3(x):
        x1 = x.astype(BF16)
        r1 = x - x1.astype(F32)
        x2 = r1.astype(BF16)
        return [x1, x2, (r1 - x2.astype(F32)).astype(BF16)]

    def stack_heads(x):
        return jnp.concatenate([jnp.where(hmask[h], x, 0.0) for h in range(H)], axis=0).astype(BF16)

    def head_diag(y, r, w):
        return jnp.concatenate([y[h * r:(h + 1) * r, h * w:(h + 1) * w] for h in range(H)], axis=1)

    q = q_ref[...] * (GLA_DK ** -0.5)
    k = k_ref[...]
    vb = v_ref[...].astype(BF16)
    a = a_ref[...]
    a_hi = a.astype(BF16)
    a_lo = (a - a_hi.astype(F32)).astype(BF16)
    xw = _dot(a_hi, wg_ref[...])
    x = xw[:, 0:B_QK] + xw[:, B_QK:2 * B_QK] + _dot(a_lo, wg_ref[:, 0:B_QK]) + bg_ref[...]
    g = (jnp.minimum(x, 0.0) - jnp.log(1.0 + jnp.exp(-jnp.abs(x)))) * (1.0 / GLA_TAU)
    cm = _dot(mat_ref[...], jnp.concatenate(split3(g), axis=1))
    b = cm[:, 0:B_QK] + cm[:, B_QK:2 * B_QK] + cm[:, 2 * B_QK:3 * B_QK]

    def group_rows(grp, row_in_group):
        return jnp.concatenate([jnp.broadcast_to(b[r0 + row_in_group:r0 + row_in_group + 1, :], (grp, B_QK))
                                for r0 in range(0, rows_t, grp)], axis=0)

    to_end = group_rows(C, 0 if reverse else C - 1) - b
    qb = q * jnp.exp(b)
    kend = (k * jnp.exp(to_end)).astype(BF16)
    q_lv, k_lv = [], []
    for lv in range(GLA_LEVELS):
        grp = C >> lv
        ref = group_rows(grp, grp // 2 if reverse else grp // 2 - 1)
        e_lv = jnp.exp(-jnp.abs(b - ref))
        q_lv.append(q * e_lv)
        k_lv.append((k * e_lv).astype(BF16))
    terms = []
    for d in range(SUB):
        sh = (rows_t - d) % rows_t if reverse else d
        kd = pltpu.roll(k, sh, 0) if sh else k
        bd = pltpu.roll(b, sh, 0) if sh else b
        terms.append((q * kd * jnp.exp(jnp.minimum(b - bd, 0.0))).astype(BF16))
    sacc = jnp.where(vmask_ref[...] != 0.0, _dot(jnp.concatenate(terms, axis=1), hsel_ref[...]), 0.0)
    level = lvl_ref[...]

    st = st_s[...]
    n_chunks = rows_t // C
    order = range(n_chunks - 1, -1, -1) if reverse else range(n_chunks)
    for c in order:
        rows = slice(c * C, (c + 1) * C)
        diag = []
        for h in range(H):
            xh = jnp.where(lane_s == h, sacc[rows, :], 0.0)
            sh = (LANES - h * SUB - (0 if reverse else SUB - 1)) % LANES
            diag.append(pltpu.roll(xh, sh, 1, stride=1, stride_axis=0))
        p = jnp.concatenate(diag, axis=0)[:, 0:C]
        for lv in range(GLA_LEVELS):
            p = jnp.where(level == lv + 1, _nt(stack_heads(q_lv[lv][rows, :]), k_lv[lv][rows, :]), p)
        o = head_diag(_dot(p.astype(BF16), vb[rows, :]), C, GLA_DV)

        o_int = _nt(stack_heads(qb[rows, :]), st.astype(BF16))
        o = o + jnp.concatenate([o_int[h * C:(h + 1) * C, :] for h in range(H)], axis=1)
        upd = lax.dot_general(vb[rows, :], kend[rows, :], (((0,), (0,)), ((), ())),
                              preferred_element_type=F32)
        new = upd[(H - 1) * GLA_DV:H * GLA_DV, :]
        for h in range(H - 2, -1, -1):
            new = jnp.where(hmask[h], upd[h * GLA_DV:(h + 1) * GLA_DV, :], new)
        r_end = c * C if reverse else (c + 1) * C - 1
        st = st * jnp.exp(b[r_end:r_end + 1, :]) + new

        if reverse:
            ot = o + of_ref[rows, :]
            outs = []
            for h in range(H):
                oh = ot[:, h * GLA_DV:(h + 1) * GLA_DV]
                ms = jnp.mean(oh * oh, axis=-1, keepdims=True)
                outs.append(oh * lax.rsqrt(ms + RMS_EPS) * ng_ref[...])
            r = r_ref[rows, :]
            o_ref[rows, :] = (jnp.concatenate(outs, axis=1) * (r * jax.nn.sigmoid(r))).astype(o_ref.dtype)
        else:
            o_ref[rows, :] = o
    st_s[...] = st


def _gla_consts(reverse):
    C, S, H = GLA_CHUNK, GLA_SUB, GLA_HEADS
    assert C == S * 2 ** GLA_LEVELS and H * S <= LANES
    t = np.arange(TM)
    chunk, tt = t // C, t % C
    tau = (C - 1 - tt) if reverse else tt
    same = chunk[:, None] == chunk[None, :]
    incl = (same & (tau[None, :] <= tau[:, None])).astype(np.float32)
    level = np.zeros((C, C), np.int32)
    for lv in range(GLA_LEVELS):
        grp = C >> lv
        later = (tau % grp) >= grp // 2
        tl, ll = tau[:C], later[:C]
        pair = (tl[:, None] // grp == tl[None, :] // grp) & ll[:, None] & ~ll[None, :]
        level[pair] = lv + 1
    hsel = np.zeros((S, B_QK, LANES), np.float32)
    for d in range(S):
        j = d if reverse else S - 1 - d
        for h in range(H):
            hsel[d, h * GLA_DK:(h + 1) * GLA_DK, h * S + j] = 1.0
    lane = np.arange(LANES)
    d_of_lane = (lane % S) if reverse else (S - 1 - lane % S)
    in_blk = ((tt % S)[:, None] + d_of_lane[None, :] <= S - 1) if reverse else ((tt % S)[:, None] >= d_of_lane[None, :])
    vmask = ((lane < H * S)[None, :] & in_blk).astype(np.float32)
    return (jnp.asarray(incl, BF16), jnp.asarray(hsel.reshape(S * B_QK, LANES), BF16),
            jnp.asarray(np.tile(level, (H, 1))), jnp.asarray(vmask))


def _gla_call(zb, o_fwd, wg, bg, ng, bsz, tb, reverse):
    t = zb.shape[0]
    nt_b = tb // TM
    consts = _gla_consts(reverse)
    const_specs = [pl.BlockSpec(c.shape, lambda b, i: (0, 0)) for c in consts]
    if reverse:
        def blk(b, i):
            return b * nt_b + jnp.where(i == 0, 0, nt_b - i)
    else:
        def blk(b, i):
            return b * nt_b + i
    qs = pl.BlockSpec((TM, B_QK), lambda b, i: (blk(b, i), 0))
    ks = pl.BlockSpec((TM, B_QK), lambda b, i: (blk(b, i), 1))
    vs = pl.BlockSpec((TM, B_V), lambda b, i: (blk(b, i), 1))
    rsp = pl.BlockSpec((TM, B_V), lambda b, i: (blk(b, i), 2))
    asp = pl.BlockSpec((TM, LANES), lambda b, i: (blk(b, i), (2 * B_QK + 2 * B_V) // LANES))
    osp = pl.BlockSpec((TM, B_V), lambda b, i: (blk(b, i), 0))
    wsp = pl.BlockSpec((LANES, 2 * B_QK), lambda b, i: (0, 0))
    bsp = pl.BlockSpec((1, B_QK), lambda b, i: (0, 0))
    if reverse:
        in_specs = [qs, ks, vs, asp, rsp, osp, wsp, bsp] + const_specs + [
            pl.BlockSpec((1, GLA_DV), lambda b, i: (0, 0))]
        args = (zb, zb, zb, zb, zb, o_fwd, wg, bg) + consts + (ng,)
        out_dtype = BF16
    else:
        in_specs = [qs, ks, vs, asp, wsp, bsp] + const_specs
        args = (zb, zb, zb, zb, wg, bg) + consts
        out_dtype = F32
    return pl.pallas_call(
        functools.partial(_gla_kernel, reverse=reverse),
        grid=(bsz, nt_b),
        in_specs=in_specs,
        out_specs=osp,
        out_shape=jax.ShapeDtypeStruct((t, B_V), out_dtype),
        scratch_shapes=[pltpu.VMEM((GLA_DV, B_QK), F32)],
        compiler_params=_cparams(("arbitrary", "arbitrary")),
        name="gla_bwd" if reverse else "gla_fwd",
    )(*args)


def _na_kernel(q_ref, k_ref, v_ref, bias_ref, o_ref, *, n_rows, ctx_len):
    i = pl.program_id(1)
    W = GRID_W
    rows_per_step = q_ref.shape[0] // W
    n_ctx_steps = ctx_len // q_ref.shape[0]
    lo_half = lax.broadcasted_iota(jnp.int32, (1, LANES), 1) < HEAD_DIM
    scale = HEAD_DIM ** -0.5
    n_pairs = C_W // LANES

    def pair_queries(q0, nq, pr):
        qp = q_ref[q0:q0 + nq, pr * LANES:(pr + 1) * LANES]
        zero = jnp.zeros_like(qp)
        return jnp.concatenate([jnp.where(lo_half, qp, zero), jnp.where(lo_half, zero, qp)], axis=0)

    def pair_out(o, nq):
        return jnp.where(lo_half, o[0:nq, :], o[nq:2 * nq, :])

    @pl.when(i < n_ctx_steps)
    def _ctx_queries():
        nq = q_ref.shape[0]
        outs = []
        for pr in range(n_pairs):
            cols = slice(pr * LANES, (pr + 1) * LANES)
            s_c = _nt(pair_queries(0, nq, pr), k_ref[0:ctx_len, cols]) * scale
            m = jnp.max(s_c, axis=-1, keepdims=True)
            p = jnp.exp(s_c - m)
            l = jnp.sum(p, axis=-1, keepdims=True)
            outs.append(pair_out(_dot(p.astype(BF16), v_ref[0:ctx_len, cols]) / l, nq))
        o_ref[...] = jnp.concatenate(outs, axis=1).astype(o_ref.dtype)

    @pl.when(i >= n_ctx_steps)
    def _grid_rows():
        for sub in range(rows_per_step):
            r = (i - n_ctx_steps) * rows_per_step + sub
            rs = jnp.clip(r - NA_KH // 2, 0, n_rows - NA_KH)
            start = pl.multiple_of(ctx_len + rs * W, W)
            win = pl.ds(start, NA_KH * W)
            outs = []
            for pr in range(n_pairs):
                cols = slice(pr * LANES, (pr + 1) * LANES)
                qs = pair_queries(sub * W, W, pr)
                s_c = _nt(qs, k_ref[0:ctx_len, cols]) * scale
                s_w = _nt(qs, k_ref[win, cols]) * scale + bias_ref[r - rs, pr * 2 * W:(pr + 1) * 2 * W, :]
                m = jnp.maximum(jnp.max(s_w, axis=-1, keepdims=True), jnp.max(s_c, axis=-1, keepdims=True))
                p_w = jnp.exp(s_w - m)
                p_c = jnp.exp(s_c - m)
                l = jnp.sum(p_w, axis=-1, keepdims=True) + jnp.sum(p_c, axis=-1, keepdims=True)
                o = _dot(p_w.astype(BF16), v_ref[win, cols]) + _dot(p_c.astype(BF16), v_ref[0:ctx_len, cols])
                outs.append(pair_out(o / l, W))
            o_ref[sub * W:(sub + 1) * W, :] = jnp.concatenate(outs, axis=1).astype(o_ref.dtype)


def _na_call(zc, bias_d, bsz, tb):
    t = zc.shape[0]
    q_rows = NA_ROWS_PER_STEP * GRID_W
    steps = tb // q_rows
    n_rows = (tb - TM) // GRID_W
    assert TM % q_rows == 0 and n_rows % NA_ROWS_PER_STEP == 0
    kern = functools.partial(_na_kernel, n_rows=n_rows, ctx_len=TM)
    return pl.pallas_call(
        kern,
        grid=(bsz, steps),
        in_specs=[pl.BlockSpec((q_rows, C_W), lambda b, i: (b * steps + i, 0)),
                  pl.BlockSpec((tb, C_W), lambda b, i: (b, 1)),
                  pl.BlockSpec((tb, C_W), lambda b, i: (b, 2)),
                  pl.BlockSpec(bias_d.shape, lambda b, i: (0, 0, 0))],
        out_specs=pl.BlockSpec((q_rows, C_W), lambda b, i: (b * steps + i, 0)),
        out_shape=jax.ShapeDtypeStruct((t, C_W), BF16),
        compiler_params=_cparams(("arbitrary", "arbitrary")),
        name="nbr_attn",
    )(zc, zc, zc, bias_d)


def _merge_kernel(x_ref, mod_ref, oa_ref, ob_ref, oc_ref, wg_ref, bg_ref, wa_ref, wb_ref, wc_ref,
                  wo_ref, lg_ref, lb_ref, wr_ref, br_ref, x1_ref, h2_ref, rt_ref, *, alpha):
    d = x_ref.shape[1]
    x = x_ref[...]
    h = _modulate(x, mod_ref[0, 0:1, :], mod_ref[0, 1:2, :]).astype(BF16)
    y = jnp.zeros_like(x)
    for bi, (o_ref, w_ref) in enumerate(((oa_ref, wa_ref), (ob_ref, wb_ref), (oc_ref, wc_ref))):
        gate = jax.nn.sigmoid(_dot(h, wg_ref[:, bi * d:(bi + 1) * d]) + bg_ref[:, bi * d:(bi + 1) * d])
        y = y + gate * _dot(o_ref[...], w_ref[...])
    y2 = _dot(y.astype(BF16), wo_ref[...])
    x1 = _ln0(alpha * x + mod_ref[0, 2:3, :] * y2) * lg_ref[...] + lb_ref[...]
    x1_ref[...] = x1
    h2 = _modulate(x1, mod_ref[0, 3:4, :], mod_ref[0, 4:5, :])
    h2_ref[...] = h2

    h2_hi = h2.astype(BF16)
    h2_lo = (h2 - h2_hi.astype(F32)).astype(BF16)
    hw = _dot(h2_hi, wr_ref[...])
    logits = hw[:, 0:LANES] + hw[:, LANES:2 * LANES] + _dot(h2_lo, wr_ref[:, 0:LANES]) + br_ref[...]
    lane = lax.broadcasted_iota(jnp.int32, logits.shape, 1)
    big = jnp.int32(1 << 20)
    neg = jnp.float32(-jnp.inf)

    def amax(vals):
        m = jnp.max(vals, axis=-1, keepdims=True)
        return m, jnp.min(jnp.where(vals == m, lane, big), axis=-1, keepdims=True)

    gl = jnp.where(lane < N_GROUPS, logits, neg)
    gmax, grp = amax(gl)
    p_grp = 1.0 / jnp.sum(jnp.exp(gl - gmax), axis=-1, keepdims=True)
    lo = N_GROUPS + grp * EXPERTS_PER_GROUP
    sl = jnp.where((lane >= lo) & (lane < lo + EXPERTS_PER_GROUP), logits, neg)
    v1, i1 = amax(sl)
    v2, i2 = amax(jnp.where(lane == i1, neg, sl))
    e2 = jnp.exp(v2 - v1)
    g1 = p_grp / (1.0 + e2)
    g2 = p_grp * e2 / (1.0 + e2)
    rt = jnp.where(lane == 0, (i1 - N_GROUPS).astype(F32),
                   jnp.where(lane == 1, (i2 - N_GROUPS).astype(F32),
                             jnp.where(lane == 2, g1, jnp.where(lane == 3, g2, 0.0))))
    rt_ref[...] = rt


def _merge_call(xs, mods, oa, ob, oc, wgate, bgate, wa, wb, wc, wo, lg, lb, wr, br, nt_b, alpha):
    t, d = xs.shape
    full = lambda shape: pl.BlockSpec(shape, lambda j: tuple(0 for _ in shape))
    row = lambda w: pl.BlockSpec((TM, w), lambda j: (j, 0))
    return pl.pallas_call(
        functools.partial(_merge_kernel, alpha=alpha),
        grid=(t // TM,),
        in_specs=[row(d), pl.BlockSpec((1, 6, d), _mod_index(nt_b, t // TM)), row(A_Q), row(B_V), row(C_W),
                  full(wgate.shape), full(bgate.shape), full(wa.shape), full(wb.shape), full(wc.shape),
                  full(wo.shape), full(lg.shape), full(lb.shape), full(wr.shape), full(br.shape)],
        out_specs=[row(d), row(d), row(LANES)],
        out_shape=[jax.ShapeDtypeStruct((t, d), F32), jax.ShapeDtypeStruct((t, d), F32),
                   jax.ShapeDtypeStruct((t, LANES), F32)],
        compiler_params=_cparams(("arbitrary",)),
        name="merge_router",
    )(xs, mods, oa, ob, oc, wgate, bgate, wa, wb, wc, wo, lg, lb, wr, br)


def _dispatch_kernel(pos_ref, h_ref, xs_in_ref, xs_ref, sem):
    del xs_in_ref

    def row_copy(tok, slot):
        return pltpu.make_async_copy(h_ref.at[pl.ds(tok, 1)], xs_ref.at[pl.ds(slot, 1)], sem)

    def issue(tok, carry):
        for kk in range(TOP_K):
            row_copy(tok, pos_ref[0, 0, TOP_K * tok + kk]).start()
        return carry

    lax.fori_loop(0, TM, issue, 0, unroll=DMA_UNROLL)
    for kk in range(TOP_K):
        pltpu.make_async_copy(h_ref, xs_ref.at[pl.ds(0, TM)], sem).wait()


def _dispatch_call(pos3, h2, n_slots):
    t, d = h2.shape
    zeros = jnp.zeros((n_slots, d), F32)
    return pl.pallas_call(
        _dispatch_kernel,
        grid=(t // TM,),
        in_specs=[pl.BlockSpec((1, 1, TOP_K * TM), lambda j: (j, 0, 0), memory_space=pltpu.SMEM),
                  pl.BlockSpec((TM, d), lambda j: (j, 0)),
                  pl.BlockSpec(memory_space=pl.ANY)],
        out_specs=pl.BlockSpec(memory_space=pl.ANY),
        out_shape=jax.ShapeDtypeStruct((n_slots, d), F32),
        scratch_shapes=[pltpu.SemaphoreType.DMA(())],
        input_output_aliases={2: 0},
        compiler_params=_cparams(("arbitrary",)),
        name="moe_dispatch",
    )(pos3, h2, zeros)


def _expert_kernel(be_ref, x_ref, w1_ref, w3_ref, w2_ref, y_ref):
    del be_ref
    xb = x_ref[...].astype(BF16)
    a = _dot(xb, w1_ref[0].astype(BF16))
    b = _dot(xb, w3_ref[0].astype(BF16))
    mid = (a * jax.nn.sigmoid(a) * b).astype(BF16)
    y_ref[...] = _dot(mid, w2_ref[0].astype(BF16))


def _expert_call(block_expert, x_slots, w1, w3, w2):
    n_slots, d = x_slots.shape
    de = w1.shape[2]
    grid_spec = pltpu.PrefetchScalarGridSpec(
        num_scalar_prefetch=1,
        grid=(n_slots // MOE_BLK,),
        in_specs=[pl.BlockSpec((MOE_BLK, d), lambda j, be: (j, 0)),
                  pl.BlockSpec((1, d, de), lambda j, be: (be[j], 0, 0)),
                  pl.BlockSpec((1, d, de), lambda j, be: (be[j], 0, 0)),
                  pl.BlockSpec((1, de, d), lambda j, be: (be[j], 0, 0))],
        out_specs=pl.BlockSpec((MOE_BLK, d), lambda j, be: (j, 0)),
    )
    return pl.pallas_call(
        _expert_kernel,
        grid_spec=grid_spec,
        out_shape=jax.ShapeDtypeStruct((n_slots, d), F32),
        compiler_params=_cparams(("arbitrary",)),
        name="moe_experts",
    )(block_expert, x_slots, w1, w3, w2)


def _combine_kernel(pos_ref, x1_ref, mod_ref, rt_ref, ys_ref, lg_ref, lb_ref, o_ref, buf, sem, *, alpha):
    def row_copy(slot, kk, tok):
        return pltpu.make_async_copy(ys_ref.at[pl.ds(slot, 1)], buf.at[kk, pl.ds(tok, 1)], sem)

    def issue(tok, carry):
        for kk in range(TOP_K):
            row_copy(pos_ref[0, 0, TOP_K * tok + kk], kk, tok).start()
        return carry

    lax.fori_loop(0, TM, issue, 0, unroll=DMA_UNROLL)
    for kk in range(TOP_K):
        pltpu.make_async_copy(ys_ref.at[pl.ds(0, TM)], buf.at[kk], sem).wait()
    rt = rt_ref[...]
    y = rt[:, 2:3] * buf[0] + rt[:, 3:4] * buf[1]
    x2 = _ln0(alpha * x1_ref[...] + mod_ref[0, 5:6, :] * y) * lg_ref[...] + lb_ref[...]
    o_ref[...] = x2


def _combine_call(pos3, x1, mods, rt, y_slots, lg, lb, nt_b, alpha, latent_only):
    t, d = x1.shape
    vec = pl.BlockSpec((1, d), lambda j: (0, 0))
    if latent_only:
        n_lat = nt_b - 1
        n_out = (t // TM // nt_b) * n_lat
        tile = lambda j: (j // n_lat) * nt_b + 1 + j % n_lat
        mod_idx = lambda j: (j // n_lat, 0, 0)
    else:
        n_out = t // TM
        tile = lambda j: j
        mod_idx = _mod_index(nt_b, t // TM)
    return pl.pallas_call(
        functools.partial(_combine_kernel, alpha=alpha),
        grid=(n_out,),
        in_specs=[pl.BlockSpec((1, 1, TOP_K * TM), lambda j: (tile(j), 0, 0), memory_space=pltpu.SMEM),
                  pl.BlockSpec((TM, d), lambda j: (tile(j), 0)),
                  pl.BlockSpec((1, 6, d), mod_idx),
                  pl.BlockSpec((TM, LANES), lambda j: (tile(j), 0)),
                  pl.BlockSpec(memory_space=pl.ANY), vec, vec],
        out_specs=pl.BlockSpec((TM, d), lambda j: (j, 0)),
        out_shape=jax.ShapeDtypeStruct((n_out * TM, d), F32),
        scratch_shapes=[pltpu.VMEM((TOP_K, TM, d), F32), pltpu.SemaphoreType.DMA(())],
        compiler_params=_cparams(("arbitrary",)),
        name="moe_combine",
    )(pos3, x1, mods, rt, y_slots, lg, lb)


def _slot_plan(rt, n_blocks):
    flat_e = rt[:, 0:TOP_K].astype(jnp.int32).reshape(-1)
    onehot = (flat_e[:, None] == jnp.arange(N_EXPERTS, dtype=jnp.int32)[None, :]).astype(jnp.int32)
    csum = jnp.cumsum(onehot, axis=0)
    counts = csum[-1]
    padded = ((counts + MOE_BLK - 1) // MOE_BLK) * MOE_BLK
    pad_end = jnp.cumsum(padded)
    pad_start = pad_end - padded
    pos = jnp.sum(onehot * (csum - 1 + pad_start[None, :]), axis=1)
    block_start = jnp.arange(n_blocks, dtype=jnp.int32) * MOE_BLK
    block_expert = jnp.minimum(
        jnp.sum((pad_end[None, :] <= block_start[:, None]).astype(jnp.int32), axis=1), N_EXPERTS - 1)
    return pos.astype(jnp.int32), block_expert.astype(jnp.int32)


def _rope_tables(seq_len, ctx_len):
    n_freq = HEAD_DIM // 4
    t = jnp.arange(seq_len)
    invf = ROPE_THETA ** (-jnp.arange(n_freq, dtype=F32) / n_freq)
    ar = (t // GRID_W).astype(F32)[:, None] * invf
    ac = (t % GRID_W).astype(F32)[:, None] * invf
    ang = jnp.concatenate([ar, ar, ac, ac], axis=-1)
    cos, sin = jnp.cos(ang), jnp.sin(ang)
    quarter = (np.arange(HEAD_DIM) // n_freq) % 2
    sa = sin * jnp.asarray(quarter == 1, F32)
    sb = -sin * jnp.asarray(quarter == 0, F32)
    ones = jnp.ones((ctx_len, HEAD_DIM), F32)
    zeros = jnp.zeros((ctx_len, HEAD_DIM), F32)
    tabs = []
    for lat, ctx in ((cos, ones), (sa, zeros), (sb, zeros)):
        tab = jnp.concatenate([ctx, lat], axis=0)
        tabs.append(jnp.tile(tab, (1, LANES // HEAD_DIM)))
    return tabs


def _na_bias_table(rpb):
    kh = NA_KH
    w = np.arange(GRID_W)
    col_start = np.clip(w - NA_KW // 2, 0, GRID_W - NA_KW)
    colk = np.arange(GRID_W)
    inside = (colk[None, :] >= col_start[:, None]) & (colk[None, :] < col_start[:, None] + NA_KW)
    col_rel = colk[None, :] - w[:, None] + (NA_KW - 1)
    pats = np.arange(kh)
    row_rel = np.arange(kh)[None, :] - pats[:, None] + (NA_KH - 1)
    row_sel = (row_rel[:, :, None] == np.arange(2 * NA_KH - 1)).astype(np.float32)
    col_sel = ((col_rel[:, :, None] == np.arange(2 * NA_KW - 1)) & inside[:, :, None]).astype(np.float32)
    b = jnp.einsum('lhab,pia,wcb->lphwic', rpb, jnp.asarray(row_sel), jnp.asarray(col_sel), precision=HIGHEST)
    b = jnp.where(jnp.asarray(inside)[None, None, None, :, None, :], b, NEG_BIG)
    return b.reshape(rpb.shape[0], kh, NA_HEADS * GRID_W, kh * GRID_W).astype(F32)


def _pack_in_proj(w_in, b_in):
    a0 = 0
    a1 = a0 + NCOL_A
    bq0 = a1
    b_end = bq0 + 2 * B_QK + 2 * B_V + B_A
    c_end = b_end + NCOL_C
    d = w_in.shape[0]
    padw = jnp.zeros((d, LANES - B_A), w_in.dtype)
    padb = jnp.zeros((LANES - B_A,), b_in.dtype)
    w_p = jnp.concatenate([w_in[:, a0:a1], w_in[:, bq0:b_end], padw, w_in[:, b_end:c_end]], axis=1)
    b_p = jnp.concatenate([b_in[a0:a1], b_in[bq0:b_end], padb, b_in[b_end:c_end]])
    return w_p.astype(BF16), b_p.reshape(1, -1), w_in[:, c_end:].astype(BF16), b_in[c_end:].reshape(1, -1)


def kernel(x, c, ctx, c_ctx, w_mod, b_mod, w_in, b_in, attn_q_norm, attn_k_norm, gla_w_gate, gla_b_gate,
           gla_norm, na_rpb, w_br_attn, w_br_gla, w_br_na, w_out, ln1_g, ln1_b, w_router_group,
           b_router_group, w_router_expert, b_router_expert, moe_w1, moe_w3, moe_w2, ln2_g, ln2_b):
    bsz, seq, d = x.shape
    ctx_len = ctx.shape[1]
    depth = w_mod.shape[0]
    assert ctx_len == TM and seq % TM == 0 and (seq // GRID_W) >= NA_KH
    tb = ctx_len + seq
    nt_b = tb // TM
    t = bsz * tb
    alpha = (2 * depth) ** 0.25

    xs = jnp.concatenate([ctx, x], axis=1).reshape(t, d)

    cin = jnp.zeros((8, d), F32).at[0:bsz].set(c).at[bsz].set(c_ctx)
    mod_all = _mod_call(cin, w_mod, b_mod)

    tabs = _rope_tables(seq, ctx_len)
    gm = jnp.asarray(np.kron(np.eye(LANES // HEAD_DIM), np.ones((HEAD_DIM, HEAD_DIM))) / HEAD_DIM, BF16)
    n_assign = t * TOP_K
    n_blocks = -(-n_assign // MOE_BLK) + N_EXPERTS
    n_slots = n_blocks * MOE_BLK
    na_bias = _na_bias_table(na_rpb)
    w1_all = moe_w1.reshape((depth * N_EXPERTS,) + moe_w1.shape[2:])
    w3_all = moe_w3.reshape((depth * N_EXPERTS,) + moe_w3.shape[2:])
    w2_all = moe_w2.reshape((depth * N_EXPERTS,) + moe_w2.shape[2:])

    for l in range(depth):
        m = mod_all[l].reshape(8, 6, d)
        mods = m

        w_p, b_p, w_gate_cols, b_gate_cols = _pack_in_proj(w_in[l], b_in[l])
        qn = jnp.tile(attn_q_norm[l], LANES // HEAD_DIM).reshape(1, LANES)
        kn = jnp.tile(attn_k_norm[l], LANES // HEAD_DIM).reshape(1, LANES)
        za, zb, zc = _proj_call(xs, mods, w_p, b_p, tabs, qn, kn, gm, nt_b)
        oa = _gqa_call(za, bsz, tb)

        o_f = None
        for direction in range(2):
            wg = jnp.zeros((LANES, B_QK), F32).at[direction * GLA_GATE_RANK:(direction + 1) * GLA_GATE_RANK].set(
                gla_w_gate[l, direction])
            wg_hi = wg.astype(BF16)
            wg = jnp.concatenate([wg_hi, (wg - wg_hi.astype(F32)).astype(BF16)], axis=1)
            bg = gla_b_gate[l, direction].reshape(1, B_QK)
            o_f = _gla_call(zb, o_f, wg, bg, gla_norm[l].reshape(1, GLA_DV), bsz, tb, reverse=direction == 1)
        ob = o_f

        oc = _na_call(zc, na_bias[l], bsz, tb)

        wr = jnp.concatenate([w_router_group[l], w_router_expert[l],
                              jnp.zeros((d, LANES - N_GROUPS - N_EXPERTS), F32)], axis=1)
        br = jnp.concatenate([b_router_group[l], b_router_expert[l],
                              jnp.zeros((LANES - N_GROUPS - N_EXPERTS,), F32)]).reshape(1, LANES)
        wr_hi = wr.astype(BF16)
        wr = jnp.concatenate([wr_hi, (wr - wr_hi.astype(F32)).astype(BF16)], axis=1)
        x1, h2, rt = _merge_call(xs, mods, oa, ob, oc, w_gate_cols, b_gate_cols,
                                 w_br_attn[l].astype(BF16), w_br_gla[l].astype(BF16), w_br_na[l].astype(BF16),
                                 w_out[l].astype(BF16), ln1_g[l].reshape(1, d), ln1_b[l].reshape(1, d),
                                 wr, br, nt_b, alpha)

        pos, block_expert = _slot_plan(rt, n_blocks)
        pos3 = pos.reshape(t // TM, 1, TOP_K * TM)
        x_slots = _dispatch_call(pos3, h2, n_slots)
        y_slots = _expert_call(block_expert + l * N_EXPERTS, x_slots, w1_all, w3_all, w2_all)
        xs = _combine_call(pos3, x1, mods, rt, y_slots, ln2_g[l].reshape(1, d), ln2_b[l].reshape(1, d),
                           nt_b, alpha, latent_only=l == depth - 1)

    return xs.reshape(bsz, seq, d)
```

```python
import functools

import jax
import jax.numpy as jnp
import numpy as np
from jax import lax
from jax.experimental import pallas as pl
from jax.experimental.pallas import tpu as pltpu

F32 = jnp.float32
BF16 = jnp.bfloat16
HIGHEST = lax.Precision.HIGHEST

GRID_W = 64
HEAD_DIM = 64
ROPE_THETA = 10000.0
ATTN_HEADS = 8
ATTN_KV_HEADS = 2
GLA_HEADS = 4
GLA_DK = 64
GLA_DV = 128
GLA_GATE_RANK = 16
GLA_TAU = 16.0
GLA_CHUNK = 64
GLA_SUB = 8
GLA_LEVELS = 3
NA_HEADS = 8
NA_KH = 8
NA_KW = 16
NA_ROWS_PER_STEP = 2
N_GROUPS = 4
EXPERTS_PER_GROUP = 8
N_EXPERTS = N_GROUPS * EXPERTS_PER_GROUP
TOP_K = 2
LN_EPS = 1e-6
RMS_EPS = 1e-6
NEG_BIG = -1e30
LOG2E = 1.4426950408889634

LANES = 128
TM = 256
MOE_BLK = 256
DMA_UNROLL = 8
VMEM_LIMIT = 56 * 1024 * 1024

A_Q, A_KV = ATTN_HEADS * HEAD_DIM, ATTN_KV_HEADS * HEAD_DIM
B_QK, B_V, B_A = GLA_HEADS * GLA_DK, GLA_HEADS * GLA_DV, 2 * GLA_GATE_RANK
C_W = NA_HEADS * HEAD_DIM
NCOL_A = A_Q + 2 * A_KV
ZA_COLS = A_Q + 3 * ATTN_KV_HEADS * LANES
NCOL_B = 2 * B_QK + 2 * B_V + LANES
NCOL_C = 3 * C_W


def _cparams(sem):
    return pltpu.CompilerParams(dimension_semantics=sem, vmem_limit_bytes=VMEM_LIMIT)


def _ln0(x):
    mu = jnp.mean(x, axis=-1, keepdims=True)
    xc = x - mu
    var = jnp.mean(xc * xc, axis=-1, keepdims=True)
    return xc * lax.rsqrt(var + LN_EPS)


def _modulate(x, shift, scale):
    return _ln0(x) * (1.0 + scale) + shift


def _nt(a, b):
    return lax.dot_general(a, b, (((1,), (1,)), ((), ())), preferred_element_type=F32)


def _dot(a, b, precision=None):
    return jnp.dot(a, b, preferred_element_type=F32, precision=precision)


def _mod_kernel(c_ref, w_ref, b_ref, o_ref):
    c = c_ref[...]
    s = c * jax.nn.sigmoid(c)
    o_ref[0] = _dot(s, w_ref[0]) + b_ref[0]


def _mod_call(cin, w_mod, b_mod):
    depth, d, nmod = w_mod.shape
    nb = nmod // d
    return pl.pallas_call(
        _mod_kernel,
        grid=(depth, nb),
        in_specs=[pl.BlockSpec((8, d), lambda l, n: (0, 0)),
                  pl.BlockSpec((1, d, d), lambda l, n: (l, 0, n)),
                  pl.BlockSpec((1, 1, d), lambda l, n: (l, 0, n))],
        out_specs=pl.BlockSpec((1, 8, d), lambda l, n: (l, 0, n)),
        out_shape=jax.ShapeDtypeStruct((depth, 8, nmod), F32),
        compiler_params=_cparams(("arbitrary", "arbitrary")),
        name="adaln_mod",
    )(cin, w_mod, b_mod.reshape(depth, 1, nmod))


def _proj_kernel(x_ref, mod_ref, w_ref, b_ref, cos_ref, sa_ref, sb_ref, qn_ref, kn_ref, gm_ref,
                 za_ref, zb_ref, zc_ref):
    h = _modulate(x_ref[...], mod_ref[0, 0:1, :], mod_ref[0, 1:2, :]).astype(BF16)
    lo = NCOL_A
    for o_ref in (zb_ref, zc_ref):
        hi = lo + o_ref.shape[1]
        acc = _dot(h, w_ref[:, lo:hi]) + b_ref[:, lo:hi]
        o_ref[...] = acc.astype(o_ref.dtype)
        lo = hi

    gm = gm_ref[...]
    lo_half = lax.broadcasted_iota(jnp.int32, (1, LANES), 1) < HEAD_DIM

    def rms(x, g):
        x2 = x * x
        hi2 = x2.astype(BF16)
        lo2 = (x2 - hi2.astype(F32)).astype(BF16)
        ms = _dot(hi2, gm) + _dot(lo2, gm)
        return x * lax.rsqrt(ms + RMS_EPS) * g

    def rope(x):
        q4 = HEAD_DIM // 4
        return x * cos_ref[...] + pltpu.roll(x, q4, 1) * sa_ref[...] + pltpu.roll(x, LANES - q4, 1) * sb_ref[...]

    def both_halves(x):
        sw = pltpu.roll(x, HEAD_DIM, 1)
        return [jnp.where(lo_half, x, sw), jnp.where(lo_half, sw, x)]

    acc = _dot(h, w_ref[:, 0:NCOL_A]) + b_ref[:, 0:NCOL_A]
    parts = []
    for i in range(A_Q // LANES):
        parts.append(rope(rms(acc[:, i * LANES:(i + 1) * LANES], qn_ref[...])) * (HEAD_DIM ** -0.5 * LOG2E))
    parts += both_halves(rope(rms(acc[:, A_Q:A_Q + A_KV], kn_ref[...])))
    for vv in both_halves(acc[:, A_Q + A_KV:A_Q + 2 * A_KV]):
        parts += [jnp.where(lo_half, vv, 1.0), jnp.where(lo_half, 1.0, vv)]
    za_ref[...] = jnp.concatenate(parts, axis=1).astype(za_ref.dtype)


def _mod_index(nt_b, n_tiles):
    ctx_row = n_tiles // nt_b
    return lambda j: (jnp.where(j % nt_b == 0, ctx_row, j // nt_b), 0, 0)


def _proj_call(xs, mods, w_p, b_p, tabs, q_norm, k_norm, gm, nt_b):
    t, d = xs.shape
    ncol = w_p.shape[1]
    tab = pl.BlockSpec((TM, LANES), lambda j: (j % nt_b, 0))
    vec = pl.BlockSpec((1, LANES), lambda j: (0, 0))
    return pl.pallas_call(
        _proj_kernel,
        grid=(t // TM,),
        in_specs=[pl.BlockSpec((TM, d), lambda j: (j, 0)),
                  pl.BlockSpec((1, 6, d), _mod_index(nt_b, t // TM)),
                  pl.BlockSpec((d, ncol), lambda j: (0, 0)),
                  pl.BlockSpec((1, ncol), lambda j: (0, 0)),
                  tab, tab, tab, vec, vec,
                  pl.BlockSpec((LANES, LANES), lambda j: (0, 0))],
        out_specs=[pl.BlockSpec((TM, ZA_COLS), lambda j: (j, 0)),
                   pl.BlockSpec((TM, NCOL_B), lambda j: (j, 0)),
                   pl.BlockSpec((TM, NCOL_C), lambda j: (j, 0))],
        out_shape=[jax.ShapeDtypeStruct((t, ZA_COLS), BF16),
                   jax.ShapeDtypeStruct((t, NCOL_B), F32),
                   jax.ShapeDtypeStruct((t, NCOL_C), BF16)],
        compiler_params=_cparams(("arbitrary",)),
        name="in_proj",
    )(xs, mods, w_p, b_p, *tabs, q_norm, k_norm, gm)


def _gqa_kernel(q_ref, k_ref, v_ref, o_ref, *, n_keys, ctx_len):
    qi = pl.program_id(2)
    lo_half = lax.broadcasted_iota(jnp.int32, (1, LANES), 1) < HEAD_DIM
    group = ATTN_HEADS // ATTN_KV_HEADS
    zero = jnp.zeros((TM, LANES), BF16)

    def attend(nk):
        outs = []
        for g in range(group):
            xq = q_ref[:, (g // 2) * LANES:(g // 2 + 1) * LANES]
            qm = jnp.where(lo_half if g % 2 == 0 else jnp.logical_not(lo_half), xq, zero)
            s = _nt(qm, k_ref[0:nk, :])
            p = jnp.exp2(s - jnp.max(s, axis=-1, keepdims=True)).astype(BF16)
            outs.append(_dot(p, v_ref[0:nk, (g % 2) * LANES:(g % 2 + 1) * LANES]))
        tiles = []
        for i in range(group // 2):
            num = jnp.where(lo_half, outs[2 * i], outs[2 * i + 1])
            den = pltpu.roll(jnp.where(lo_half, outs[2 * i + 1], outs[2 * i]), HEAD_DIM, 1)
            tiles.append(num / den)
        o_ref[...] = jnp.concatenate(tiles, axis=1).astype(o_ref.dtype)

    pl.when(qi == 0)(lambda: attend(ctx_len))
    pl.when(qi > 0)(lambda: attend(n_keys))


def _gqa_call(za, bsz, tb):
    t = za.shape[0]
    nt_b = tb // TM
    k_blk = A_Q // LANES
    return pl.pallas_call(
        functools.partial(_gqa_kernel, n_keys=tb, ctx_len=TM),
        grid=(bsz, ATTN_KV_HEADS, nt_b),
        in_specs=[pl.BlockSpec((TM, 2 * LANES), lambda b, h, i: (b * nt_b + i, h)),
                  pl.BlockSpec((tb, LANES), lambda b, h, i: (b, k_blk + h)),
                  pl.BlockSpec((tb, 2 * LANES), lambda b, h, i: (b, (k_blk + ATTN_KV_HEADS) // 2 + h))],
        out_specs=pl.BlockSpec((TM, 2 * LANES), lambda b, h, i: (b * nt_b + i, h)),
        out_shape=jax.ShapeDtypeStruct((t, A_Q), BF16),
        compiler_params=_cparams(("arbitrary", "arbitrary", "arbitrary")),
        name="gqa_axial",
    )(za, za, za)


def _gla_kernel(*refs, reverse):
    if reverse:
        (q_ref, k_ref, v_ref, a_ref, r_ref, of_ref, wg_ref, bg_ref, mat_ref, hsel_ref, lvl_ref, vmask_ref,
         ng_ref, o_ref, st_s) = refs
    else:
        q_ref, k_ref, v_ref, a_ref, wg_ref, bg_ref, mat_ref, hsel_ref, lvl_ref, vmask_ref, o_ref, st_s = refs
    C, SUB, H = GLA_CHUNK, GLA_SUB, GLA_HEADS
    rows_t = q_ref.shape[0]

    @pl.when(pl.program_id(1) == 0)
    def _zero_state():
        st_s[...] = jnp.zeros_like(st_s)

    lane_k = lax.broadcasted_iota(jnp.int32, (1, B_QK), 1) // GLA_DK
    hmask = [lane_k == h for h in range(H)]
    lane_s = lax.broadcasted_iota(jnp.int32, (1, LANES), 1) // SUB

    def split3(x):
        x1 = x.astype(BF16)
        r1 = x - x1.astype(F32)
        x2 = r1.astype(BF16)
        return [x1, x2, (r1 - x2.astype(F32)).astype(BF16)]

    def stack_heads(x):
        return jnp.concatenate([jnp.where(hmask[h], x, 0.0) for h in range(H)], axis=0).astype(BF16)

    def head_diag(y, r, w):
        return jnp.concatenate([y[h * r:(h + 1) * r, h * w:(h + 1) * w] for h in range(H)], axis=1)

    q = q_ref[...] * (GLA_DK ** -0.5)
    k = k_ref[...]
    vb = v_ref[...].astype(BF16)
    a = a_ref[...]
    a_hi = a.astype(BF16)
    a_lo = (a - a_hi.astype(F32)).astype(BF16)
    xw = _dot(a_hi, wg_ref[...])
    x = xw[:, 0:B_QK] + xw[:, B_QK:2 * B_QK] + _dot(a_lo, wg_ref[:, 0:B_QK]) + bg_ref[...]
    g = (jnp.minimum(x, 0.0) - jnp.log(1.0 + jnp.exp(-jnp.abs(x)))) * (1.0 / GLA_TAU)
    cm = _dot(mat_ref[...], jnp.concatenate(split3(g), axis=1))
    b = cm[:, 0:B_QK] + cm[:, B_QK:2 * B_QK] + cm[:, 2 * B_QK:3 * B_QK]

    def group_rows(grp, row_in_group):
        return jnp.concatenate([jnp.broadcast_to(b[r0 + row_in_group:r0 + row_in_group + 1, :], (grp, B_QK))
                                for r0 in range(0, rows_t, grp)], axis=0)

    to_end = group_rows(C, 0 if reverse else C - 1) - b
    qb = q * jnp.exp(b)
    kend = (k * jnp.exp(to_end)).astype(BF16)
    q_lv, k_lv = [], []
    for lv in range(GLA_LEVELS):
        grp = C >> lv
        ref = group_rows(grp, grp // 2 if reverse else grp // 2 - 1)
        e_lv = jnp.exp(-jnp.abs(b - ref))
        q_lv.append(q * e_lv)
        k_lv.append((k * e_lv).astype(BF16))
    terms = []
    for d in range(SUB):
        sh = (rows_t - d) % rows_t if reverse else d
        kd = pltpu.roll(k, sh, 0) if sh else k
        bd = pltpu.roll(b, sh, 0) if sh else b
        terms.append((q * kd * jnp.exp(jnp.minimum(b - bd, 0.0))).astype(BF16))
    sacc = jnp.where(vmask_ref[...] != 0.0, _dot(jnp.concatenate(terms, axis=1), hsel_ref[...]), 0.0)
    level = lvl_ref[...]

    st = st_s[...]
    n_chunks = rows_t // C
    order = range(n_chunks - 1, -1, -1) if reverse else range(n_chunks)
    for c in order:
        rows = slice(c * C, (c + 1) * C)
        diag = []
        for h in range(H):
            xh = jnp.where(lane_s == h, sacc[rows, :], 0.0)
            sh = (LANES - h * SUB - (0 if reverse else SUB - 1)) % LANES
            diag.append(pltpu.roll(xh, sh, 1, stride=1, stride_axis=0))
        p = jnp.concatenate(diag, axis=0)[:, 0:C]
        for lv in range(GLA_LEVELS):
            p = jnp.where(level == lv + 1, _nt(stack_heads(q_lv[lv][rows, :]), k_lv[lv][rows, :]), p)
        o = head_diag(_dot(p.astype(BF16), vb[rows, :]), C, GLA_DV)

        o_int = _nt(stack_heads(qb[rows, :]), st.astype(BF16))
        o = o + jnp.concatenate([o_int[h * C:(h + 1) * C, :] for h in range(H)], axis=1)
        upd = lax.dot_general(vb[rows, :], kend[rows, :], (((0,), (0,)), ((), ())),
                              preferred_element_type=F32)
        new = upd[(H - 1) * GLA_DV:H * GLA_DV, :]
        for h in range(H - 2, -1, -1):
            new = jnp.where(hmask[h], upd[h * GLA_DV:(h + 1) * GLA_DV, :], new)
        r_end = c * C if reverse else (c + 1) * C - 1
        st = st * jnp.exp(b[r_end:r_end + 1, :]) + new

        if reverse:
            ot = o + of_ref[rows, :]
            outs = []
            for h in range(H):
                oh = ot[:, h * GLA_DV:(h + 1) * GLA_DV]
                ms = jnp.mean(oh * oh, axis=-1, keepdims=True)
                outs.append(oh * lax.rsqrt(ms + RMS_EPS) * ng_ref[...])
            r = r_ref[rows, :]
            o_ref[rows, :] = (jnp.concatenate(outs, axis=1) * (r * jax.nn.sigmoid(r))).astype(o_ref.dtype)
        else:
            o_ref[rows, :] = o
    st_s[...] = st


def _gla_consts(reverse):
    C, S, H = GLA_CHUNK, GLA_SUB, GLA_HEADS
    assert C == S * 2 ** GLA_LEVELS and H * S <= LANES
    t = np.arange(TM)
    chunk, tt = t // C, t % C
    tau = (C - 1 - tt) if reverse else tt
    same = chunk[:, None] == chunk[None, :]
    incl = (same & (tau[None, :] <= tau[:, None])).astype(np.float32)
    level = np.zeros((C, C), np.int32)
    for lv in range(GLA_LEVELS):
        grp = C >> lv
        later = (tau % grp) >= grp // 2
        tl, ll = tau[:C], later[:C]
        pair = (tl[:, None] // grp == tl[None, :] // grp) & ll[:, None] & ~ll[None, :]
        level[pair] = lv + 1
    hsel = np.zeros((S, B_QK, LANES), np.float32)
    for d in range(S):
        j = d if reverse else S - 1 - d
        for h in range(H):
            hsel[d, h * GLA_DK:(h + 1) * GLA_DK, h * S + j] = 1.0
    lane = np.arange(LANES)
    d_of_lane = (lane % S) if reverse else (S - 1 - lane % S)
    in_blk = ((tt % S)[:, None] + d_of_lane[None, :] <= S - 1) if reverse else ((tt % S)[:, None] >= d_of_lane[None, :])
    vmask = ((lane < H * S)[None, :] & in_blk).astype(np.float32)
    return (jnp.asarray(incl, BF16), jnp.asarray(hsel.reshape(S * B_QK, LANES), BF16),
            jnp.asarray(np.tile(level, (H, 1))), jnp.asarray(vmask))


def _gla_call(zb, o_fwd, wg, bg, ng, bsz, tb, reverse):
    t = zb.shape[0]
    nt_b = tb // TM
    consts = _gla_consts(reverse)
    const_specs = [pl.BlockSpec(c.shape, lambda b, i: (0, 0)) for c in consts]
    if reverse:
        def blk(b, i):
            return b * nt_b + jnp.where(i == 0, 0, nt_b - i)
    else:
        def blk(b, i):
            return b * nt_b + i
    qs = pl.BlockSpec((TM, B_QK), lambda b, i: (blk(b, i), 0))
    ks = pl.BlockSpec((TM, B_QK), lambda b, i: (blk(b, i), 1))
    vs = pl.BlockSpec((TM, B_V), lambda b, i: (blk(b, i), 1))
    rsp = pl.BlockSpec((TM, B_V), lambda b, i: (blk(b, i), 2))
    asp = pl.BlockSpec((TM, LANES), lambda b, i: (blk(b, i), (2 * B_QK + 2 * B_V) // LANES))
    osp = pl.BlockSpec((TM, B_V), lambda b, i: (blk(b, i), 0))
    wsp = pl.BlockSpec((LANES, 2 * B_QK), lambda b, i: (0, 0))
    bsp = pl.BlockSpec((1, B_QK), lambda b, i: (0, 0))
    if reverse:
        in_specs = [qs, ks, vs, asp, rsp, osp, wsp, bsp] + const_specs + [
            pl.BlockSpec((1, GLA_DV), lambda b, i: (0, 0))]
        args = (zb, zb, zb, zb, zb, o_fwd, wg, bg) + consts + (ng,)
        out_dtype = BF16
    else:
        in_specs = [qs, ks, vs, asp, wsp, bsp] + const_specs
        args = (zb, zb, zb, zb, wg, bg) + consts
        out_dtype = F32
    return pl.pallas_call(
        functools.partial(_gla_kernel, reverse=reverse),
        grid=(bsz, nt_b),
        in_specs=in_specs,
        out_specs=osp,
        out_shape=jax.ShapeDtypeStruct((t, B_V), out_dtype),
        scratch_shapes=[pltpu.VMEM((GLA_DV, B_QK), F32)],
        compiler_params=_cparams(("arbitrary", "arbitrary")),
        name="gla_bwd" if reverse else "gla_fwd",
    )(*args)


def _na_kernel(q_ref, k_ref, v_ref, bias_ref, o_ref, *, n_rows, ctx_len):
    i = pl.program_id(1)
    W = GRID_W
    rows_per_step = q_ref.shape[0] // W
    n_ctx_steps = ctx_len // q_ref.shape[0]
    lo_half = lax.broadcasted_iota(jnp.int32, (1, LANES), 1) < HEAD_DIM
    scale = HEAD_DIM ** -0.5
    n_pairs = C_W // LANES

    def pair_queries(q0, nq, pr):
        qp = q_ref[q0:q0 + nq, pr * LANES:(pr + 1) * LANES]
        zero = jnp.zeros_like(qp)
        return jnp.concatenate([jnp.where(lo_half, qp, zero), jnp.where(lo_half, zero, qp)], axis=0)

    def pair_out(o, nq):
        return jnp.where(lo_half, o[0:nq, :], o[nq:2 * nq, :])

    @pl.when(i < n_ctx_steps)
    def _ctx_queries():
        nq = q_ref.shape[0]
        outs = []
        for pr in range(n_pairs):
            cols = slice(pr * LANES, (pr + 1) * LANES)
            s_c = _nt(pair_queries(0, nq, pr), k_ref[0:ctx_len, cols]) * scale
            m = jnp.max(s_c, axis=-1, keepdims=True)
            p = jnp.exp(s_c - m)
            l = jnp.sum(p, axis=-1, keepdims=True)
            outs.append(pair_out(_dot(p.astype(BF16), v_ref[0:ctx_len, cols]) / l, nq))
        o_ref[...] = jnp.concatenate(outs, axis=1).astype(o_ref.dtype)

    @pl.when(i >= n_ctx_steps)
    def _grid_rows():
        for sub in range(rows_per_step):
            r = (i - n_ctx_steps) * rows_per_step + sub
            rs = jnp.clip(r - NA_KH // 2, 0, n_rows - NA_KH)
            start = pl.multiple_of(ctx_len + rs * W, W)
            win = pl.ds(start, NA_KH * W)
            outs = []
            for pr in range(n_pairs):
                cols = slice(pr * LANES, (pr + 1) * LANES)
                qs = pair_queries(sub * W, W, pr)
                s_c = _nt(qs, k_ref[0:ctx_len, cols]) * scale
                s_w = _nt(qs, k_ref[win, cols]) * scale + bias_ref[r - rs, pr * 2 * W:(pr + 1) * 2 * W, :]
                m = jnp.maximum(jnp.max(s_w, axis=-1, keepdims=True), jnp.max(s_c, axis=-1, keepdims=True))
                p_w = jnp.exp(s_w - m)
                p_c = jnp.exp(s_c - m)
                l = jnp.sum(p_w, axis=-1, keepdims=True) + jnp.sum(p_c, axis=-1, keepdims=True)
                o = _dot(p_w.astype(BF16), v_ref[win, cols]) + _dot(p_c.astype(BF16), v_ref[0:ctx_len, cols])
                outs.append(pair_out(o / l, W))
            o_ref[sub * W:(sub + 1) * W, :] = jnp.concatenate(outs, axis=1).astype(o_ref.dtype)


def _na_call(zc, bias_d, bsz, tb):
    t = zc.shape[0]
    q_rows = NA_ROWS_PER_STEP * GRID_W
    steps = tb // q_rows
    n_rows = (tb - TM) // GRID_W
    assert TM % q_rows == 0 and n_rows % NA_ROWS_PER_STEP == 0
    kern = functools.partial(_na_kernel, n_rows=n_rows, ctx_len=TM)
    return pl.pallas_call(
        kern,
        grid=(bsz, steps),
        in_specs=[pl.BlockSpec((q_rows, C_W), lambda b, i: (b * steps + i, 0)),
                  pl.BlockSpec((tb, C_W), lambda b, i: (b, 1)),
                  pl.BlockSpec((tb, C_W), lambda b, i: (b, 2)),
                  pl.BlockSpec(bias_d.shape, lambda b, i: (0, 0, 0))],
        out_specs=pl.BlockSpec((q_rows, C_W), lambda b, i: (b * steps + i, 0)),
        out_shape=jax.ShapeDtypeStruct((t, C_W), BF16),
        compiler_params=_cparams(("arbitrary", "arbitrary")),
        name="nbr_attn",
    )(zc, zc, zc, bias_d)


def _merge_kernel(x_ref, mod_ref, oa_ref, ob_ref, oc_ref, wg_ref, bg_ref, wa_ref, wb_ref, wc_ref,
                  wo_ref, lg_ref, lb_ref, wr_ref, br_ref, x1_ref, h2_ref, rt_ref, *, alpha):
    d = x_ref.shape[1]
    x = x_ref[...]
    h = _modulate(x, mod_ref[0, 0:1, :], mod_ref[0, 1:2, :]).astype(BF16)
    y = jnp.zeros_like(x)
    for bi, (o_ref, w_ref) in enumerate(((oa_ref, wa_ref), (ob_ref, wb_ref), (oc_ref, wc_ref))):
        gate = jax.nn.sigmoid(_dot(h, wg_ref[:, bi * d:(bi + 1) * d]) + bg_ref[:, bi * d:(bi + 1) * d])
        y = y + gate * _dot(o_ref[...], w_ref[...])
    y2 = _dot(y.astype(BF16), wo_ref[...])
    x1 = _ln0(alpha * x + mod_ref[0, 2:3, :] * y2) * lg_ref[...] + lb_ref[...]
    x1_ref[...] = x1
    h2 = _modulate(x1, mod_ref[0, 3:4, :], mod_ref[0, 4:5, :])
    h2_ref[...] = h2

    h2_hi = h2.astype(BF16)
    h2_lo = (h2 - h2_hi.astype(F32)).astype(BF16)
    hw = _dot(h2_hi, wr_ref[...])
    logits = hw[:, 0:LANES] + hw[:, LANES:2 * LANES] + _dot(h2_lo, wr_ref[:, 0:LANES]) + br_ref[...]
    lane = lax.broadcasted_iota(jnp.int32, logits.shape, 1)
    big = jnp.int32(1 << 20)
    neg = jnp.float32(-jnp.inf)

    def amax(vals):
        m = jnp.max(vals, axis=-1, keepdims=True)
        return m, jnp.min(jnp.where(vals == m, lane, big), axis=-1, keepdims=True)

    gl = jnp.where(lane < N_GROUPS, logits, neg)
    gmax, grp = amax(gl)
    p_grp = 1.0 / jnp.sum(jnp.exp(gl - gmax), axis=-1, keepdims=True)
    lo = N_GROUPS + grp * EXPERTS_PER_GROUP
    sl = jnp.where((lane >= lo) & (lane < lo + EXPERTS_PER_GROUP), logits, neg)
    v1, i1 = amax(sl)
    v2, i2 = amax(jnp.where(lane == i1, neg, sl))
    e2 = jnp.exp(v2 - v1)
    g1 = p_grp / (1.0 + e2)
    g2 = p_grp * e2 / (1.0 + e2)
    rt = jnp.where(lane == 0, (i1 - N_GROUPS).astype(F32),
                   jnp.where(lane == 1, (i2 - N_GROUPS).astype(F32),
                             jnp.where(lane == 2, g1, jnp.where(lane == 3, g2, 0.0))))
    rt_ref[...] = rt


def _merge_call(xs, mods, oa, ob, oc, wgate, bgate, wa, wb, wc, wo, lg, lb, wr, br, nt_b, alpha):
    t, d = xs.shape
    full = lambda shape: pl.BlockSpec(shape, lambda j: tuple(0 for _ in shape))
    row = lambda w: pl.BlockSpec((TM, w), lambda j: (j, 0))
    return pl.pallas_call(
        functools.partial(_merge_kernel, alpha=alpha),
        grid=(t // TM,),
        in_specs=[row(d), pl.BlockSpec((1, 6, d), _mod_index(nt_b, t // TM)), row(A_Q), row(B_V), row(C_W),
                  full(wgate.shape), full(bgate.shape), full(wa.shape), full(wb.shape), full(wc.shape),
                  full(wo.shape), full(lg.shape), full(lb.shape), full(wr.shape), full(br.shape)],
        out_specs=[row(d), row(d), row(LANES)],
        out_shape=[jax.ShapeDtypeStruct((t, d), F32), jax.ShapeDtypeStruct((t, d), F32),
                   jax.ShapeDtypeStruct((t, LANES), F32)],
        compiler_params=_cparams(("arbitrary",)),
        name="merge_router",
    )(xs, mods, oa, ob, oc, wgate, bgate, wa, wb, wc, wo, lg, lb, wr, br)


def _dispatch_kernel(pos_ref, h_ref, xs_in_ref, xs_ref, sem):
    del xs_in_ref

    def row_copy(tok, slot):
        return pltpu.make_async_copy(h_ref.at[pl.ds(tok, 1)], xs_ref.at[pl.ds(slot, 1)], sem)

    def issue(tok, carry):
        for kk in range(TOP_K):
            row_copy(tok, pos_ref[0, 0, TOP_K * tok + kk]).start()
        return carry

    lax.fori_loop(0, TM, issue, 0, unroll=DMA_UNROLL)
    for kk in range(TOP_K):
        pltpu.make_async_copy(h_ref, xs_ref.at[pl.ds(0, TM)], sem).wait()


def _dispatch_call(pos3, h2, n_slots):
    t, d = h2.shape
    zeros = jnp.zeros((n_slots, d), F32)
    return pl.pallas_call(
        _dispatch_kernel,
        grid=(t // TM,),
        in_specs=[pl.BlockSpec((1, 1, TOP_K * TM), lambda j: (j, 0, 0), memory_space=pltpu.SMEM),
                  pl.BlockSpec((TM, d), lambda j: (j, 0)),
                  pl.BlockSpec(memory_space=pl.ANY)],
        out_specs=pl.BlockSpec(memory_space=pl.ANY),
        out_shape=jax.ShapeDtypeStruct((n_slots, d), F32),
        scratch_shapes=[pltpu.SemaphoreType.DMA(())],
        input_output_aliases={2: 0},
        compiler_params=_cparams(("arbitrary",)),
        name="moe_dispatch",
    )(pos3, h2, zeros)


def _expert_kernel(be_ref, x_ref, w1_ref, w3_ref, w2_ref, y_ref):
    del be_ref
    xb = x_ref[...].astype(BF16)
    a = _dot(xb, w1_ref[0].astype(BF16))
    b = _dot(xb, w3_ref[0].astype(BF16))
    mid = (a * jax.nn.sigmoid(a) * b).astype(BF16)
    y_ref[...] = _dot(mid, w2_ref[0].astype(BF16))


def _expert_call(block_expert, x_slots, w1, w3, w2):
    n_slots, d = x_slots.shape
    de = w1.shape[2]
    grid_spec = pltpu.PrefetchScalarGridSpec(
        num_scalar_prefetch=1,
        grid=(n_slots // MOE_BLK,),
        in_specs=[pl.BlockSpec((MOE_BLK, d), lambda j, be: (j, 0)),
                  pl.BlockSpec((1, d, de), lambda j, be: (be[j], 0, 0)),
                  pl.BlockSpec((1, d, de), lambda j, be: (be[j], 0, 0)),
                  pl.BlockSpec((1, de, d), lambda j, be: (be[j], 0, 0))],
        out_specs=pl.BlockSpec((MOE_BLK, d), lambda j, be: (j, 0)),
    )
    return pl.pallas_call(
        _expert_kernel,
        grid_spec=grid_spec,
        out_shape=jax.ShapeDtypeStruct((n_slots, d), F32),
        compiler_params=_cparams(("arbitrary",)),
        name="moe_experts",
    )(block_expert, x_slots, w1, w3, w2)


def _combine_kernel(pos_ref, x1_ref, mod_ref, rt_ref, ys_ref, lg_ref, lb_ref, o_ref, buf, sem, *, alpha):
    def row_copy(slot, kk, tok):
        return pltpu.make_async_copy(ys_ref.at[pl.ds(slot, 1)], buf.at[kk, pl.ds(tok, 1)], sem)

    def issue(tok, carry):
        for kk in range(TOP_K):
            row_copy(pos_ref[0, 0, TOP_K * tok + kk], kk, tok).start()
        return carry

    lax.fori_loop(0, TM, issue, 0, unroll=DMA_UNROLL)
    for kk in range(TOP_K):
        pltpu.make_async_copy(ys_ref.at[pl.ds(0, TM)], buf.at[kk], sem).wait()
    rt = rt_ref[...]
    y = rt[:, 2:3] * buf[0] + rt[:, 3:4] * buf[1]
    x2 = _ln0(alpha * x1_ref[...] + mod_ref[0, 5:6, :] * y) * lg_ref[...] + lb_ref[...]
    o_ref[...] = x2


def _combine_call(pos3, x1, mods, rt, y_slots, lg, lb, nt_b, alpha, latent_only):
    t, d = x1.shape
    vec = pl.BlockSpec((1, d), lambda j: (0, 0))
    if latent_only:
        n_lat = nt_b - 1
        n_out = (t // TM // nt_b) * n_lat
        tile = lambda j: (j // n_lat) * nt_b + 1 + j % n_lat
        mod_idx = lambda j: (j // n_lat, 0, 0)
    else:
        n_out = t // TM
        tile = lambda j: j
        mod_idx = _mod_index(nt_b, t // TM)
    return pl.pallas_call(
        functools.partial(_combine_kernel, alpha=alpha),
        grid=(n_out,),
        in_specs=[pl.BlockSpec((1, 1, TOP_K * TM), lambda j: (tile(j), 0, 0), memory_space=pltpu.SMEM),
                  pl.BlockSpec((TM, d), lambda j: (tile(j), 0)),
                  pl.BlockSpec((1, 6, d), mod_idx),
                  pl.BlockSpec((TM, LANES), lambda j: (tile(j), 0)),
                  pl.BlockSpec(memory_space=pl.ANY), vec, vec],
        out_specs=pl.BlockSpec((TM, d), lambda j: (j, 0)),
        out_shape=jax.ShapeDtypeStruct((n_out * TM, d), F32),
        scratch_shapes=[pltpu.VMEM((TOP_K, TM, d), F32), pltpu.SemaphoreType.DMA(())],
        compiler_params=_cparams(("arbitrary",)),
        name="moe_combine",
    )(pos3, x1, mods, rt, y_slots, lg, lb)


def _slot_plan(rt, n_blocks):
    flat_e = rt[:, 0:TOP_K].astype(jnp.int32).reshape(-1)
    onehot = (flat_e[:, None] == jnp.arange(N_EXPERTS, dtype=jnp.int32)[None, :]).astype(jnp.int32)
    csum = jnp.cumsum(onehot, axis=0)
    counts = csum[-1]
    padded = ((counts + MOE_BLK - 1) // MOE_BLK) * MOE_BLK
    pad_end = jnp.cumsum(padded)
    pad_start = pad_end - padded
    pos = jnp.sum(onehot * (csum - 1 + pad_start[None, :]), axis=1)
    block_start = jnp.arange(n_blocks, dtype=jnp.int32) * MOE_BLK
    block_expert = jnp.minimum(
        jnp.sum((pad_end[None, :] <= block_start[:, None]).astype(jnp.int32), axis=1), N_EXPERTS - 1)
    return pos.astype(jnp.int32), block_expert.astype(jnp.int32)


def _rope_tables(seq_len, ctx_len):
    n_freq = HEAD_DIM // 4
    t = jnp.arange(seq_len)
    invf = ROPE_THETA ** (-jnp.arange(n_freq, dtype=F32) / n_freq)
    ar = (t // GRID_W).astype(F32)[:, None] * invf
    ac = (t % GRID_W).astype(F32)[:, None] * invf
    ang = jnp.concatenate([ar, ar, ac, ac], axis=-1)
    cos, sin = jnp.cos(ang), jnp.sin(ang)
    quarter = (np.arange(HEAD_DIM) // n_freq) % 2
    sa = sin * jnp.asarray(quarter == 1, F32)
    sb = -sin * jnp.asarray(quarter == 0, F32)
    ones = jnp.ones((ctx_len, HEAD_DIM), F32)
    zeros = jnp.zeros((ctx_len, HEAD_DIM), F32)
    tabs = []
    for lat, ctx in ((cos, ones), (sa, zeros), (sb, zeros)):
        tab = jnp.concatenate([ctx, lat], axis=0)
        tabs.append(jnp.tile(tab, (1, LANES // HEAD_DIM)))
    return tabs


def _na_bias_table(rpb):
    kh = NA_KH
    w = np.arange(GRID_W)
    col_start = np.clip(w - NA_KW // 2, 0, GRID_W - NA_KW)
    colk = np.arange(GRID_W)
    inside = (colk[None, :] >= col_start[:, None]) & (colk[None, :] < col_start[:, None] + NA_KW)
    col_rel = colk[None, :] - w[:, None] + (NA_KW - 1)
    pats = np.arange(kh)
    row_rel = np.arange(kh)[None, :] - pats[:, None] + (NA_KH - 1)
    row_sel = (row_rel[:, :, None] == np.arange(2 * NA_KH - 1)).astype(np.float32)
    col_sel = ((col_rel[:, :, None] == np.arange(2 * NA_KW - 1)) & inside[:, :, None]).astype(np.float32)
    b = jnp.einsum('lhab,pia,wcb->lphwic', rpb, jnp.asarray(row_sel), jnp.asarray(col_sel), precision=HIGHEST)
    b = jnp.where(jnp.asarray(inside)[None, None, None, :, None, :], b, NEG_BIG)
    return b.reshape(rpb.shape[0], kh, NA_HEADS * GRID_W, kh * GRID_W).astype(F32)


def _pack_in_proj(w_in, b_in):
    a0 = 0
    a1 = a0 + NCOL_A
    bq0 = a1
    b_end = bq0 + 2 * B_QK + 2 * B_V + B_A
    c_end = b_end + NCOL_C
    d = w_in.shape[0]
    padw = jnp.zeros((d, LANES - B_A), w_in.dtype)
    padb = jnp.zeros((LANES - B_A,), b_in.dtype)
    w_p = jnp.concatenate([w_in[:, a0:a1], w_in[:, bq0:b_end], padw, w_in[:, b_end:c_end]], axis=1)
    b_p = jnp.concatenate([b_in[a0:a1], b_in[bq0:b_end], padb, b_in[b_end:c_end]])
    return w_p.astype(BF16), b_p.reshape(1, -1), w_in[:, c_end:].astype(BF16), b_in[c_end:].reshape(1, -1)


def kernel(x, c, ctx, c_ctx, w_mod, b_mod, w_in, b_in, attn_q_norm, attn_k_norm, gla_w_gate, gla_b_gate,
           gla_norm, na_rpb, w_br_attn, w_br_gla, w_br_na, w_out, ln1_g, ln1_b, w_router_group,
           b_router_group, w_router_expert, b_router_expert, moe_w1, moe_w3, moe_w2, ln2_g, ln2_b):
    bsz, seq, d = x.shape
    ctx_len = ctx.shape[1]
    depth = w_mod.shape[0]
    assert ctx_len == TM and seq % TM == 0 and (seq // GRID_W) >= NA_KH
    tb = ctx_len + seq
    nt_b = tb // TM
    t = bsz * tb
    alpha = (2 * depth) ** 0.25

    xs = jnp.concatenate([ctx, x], axis=1).reshape(t, d)

    cin = jnp.zeros((8, d), F32).at[0:bsz].set(c).at[bsz].set(c_ctx)
    mod_all = _mod_call(cin, w_mod, b_mod)

    tabs = _rope_tables(seq, ctx_len)
    gm = jnp.asarray(np.kron(np.eye(LANES // HEAD_DIM), np.ones((HEAD_DIM, HEAD_DIM))) / HEAD_DIM, BF16)
    n_assign = t * TOP_K
    n_blocks = -(-n_assign // MOE_BLK) + N_EXPERTS
    n_slots = n_blocks * MOE_BLK
    na_bias = _na_bias_table(na_rpb)
    w1_all = moe_w1.reshape((depth * N_EXPERTS,) + moe_w1.shape[2:])
    w3_all = moe_w3.reshape((depth * N_EXPERTS,) + moe_w3.shape[2:])
    w2_all = moe_w2.reshape((depth * N_EXPERTS,) + moe_w2.shape[2:])

    for l in range(depth):
        m = mod_all[l].reshape(8, 6, d)
        mods = m

        w_p, b_p, w_gate_cols, b_gate_cols = _pack_in_proj(w_in[l], b_in[l])
        qn = jnp.tile(attn_q_norm[l], LANES // HEAD_DIM).reshape(1, LANES)
        kn = jnp.tile(attn_k_norm[l], LANES // HEAD_DIM).reshape(1, LANES)
        za, zb, zc = _proj_call(xs, mods, w_p, b_p, tabs, qn, kn, gm, nt_b)
        oa = _gqa_call(za, bsz, tb)

        o_f = None
        for direction in range(2):
            wg = jnp.zeros((LANES, B_QK), F32).at[direction * GLA_GATE_RANK:(direction + 1) * GLA_GATE_RANK].set(
                gla_w_gate[l, direction])
            wg_hi = wg.astype(BF16)
            wg = jnp.concatenate([wg_hi, (wg - wg_hi.astype(F32)).astype(BF16)], axis=1)
            bg = gla_b_gate[l, direction].reshape(1, B_QK)
            o_f = _gla_call(zb, o_f, wg, bg, gla_norm[l].reshape(1, GLA_DV), bsz, tb, reverse=direction == 1)
        ob = o_f

        oc = _na_call(zc, na_bias[l], bsz, tb)

        wr = jnp.concatenate([w_router_group[l], w_router_expert[l],
                              jnp.zeros((d, LANES - N_GROUPS - N_EXPERTS), F32)], axis=1)
        br = jnp.concatenate([b_router_group[l], b_router_expert[l],
                              jnp.zeros((LANES - N_GROUPS - N_EXPERTS,), F32)]).reshape(1, LANES)
        wr_hi = wr.astype(BF16)
        wr = jnp.concatenate([wr_hi, (wr - wr_hi.astype(F32)).astype(BF16)], axis=1)
        x1, h2, rt = _merge_call(xs, mods, oa, ob, oc, w_gate_cols, b_gate_cols,
                                 w_br_attn[l].astype(BF16), w_br_gla[l].astype(BF16), w_br_na[l].astype(BF16),
                                 w_out[l].astype(BF16), ln1_g[l].reshape(1, d), ln1_b[l].reshape(1, d),
                                 wr, br, nt_b, alpha)

        pos, block_expert = _slot_plan(rt, n_blocks)
        pos3 = pos.reshape(t // TM, 1, TOP_K * TM)
        x_slots = _dispatch_call(pos3, h2, n_slots)
        y_slots = _expert_call(block_expert + l * N_EXPERTS, x_slots, w1_all, w3_all, w2_all)
        xs = _combine_call(pos3, x1, mods, rt, y_slots, ln2_g[l].reshape(1, d), ln2_b[l].reshape(1, d),
                           nt_b, alpha, latent_only=l == depth - 1)

    return xs.reshape(bsz, seq, d)
```

```python
import functools

import jax
import jax.numpy as jnp
import numpy as np
from jax import lax
from jax.experimental import pallas as pl
from jax.experimental.pallas import tpu as pltpu

F32 = jnp.float32
BF16 = jnp.bfloat16
HIGHEST = lax.Precision.HIGHEST

GRID_W = 64
HEAD_DIM = 64
ROPE_THETA = 10000.0
ATTN_HEADS = 8
ATTN_KV_HEADS = 2
GLA_HEADS = 4
GLA_DK = 64
GLA_DV = 128
GLA_GATE_RANK = 16
GLA_TAU = 16.0
GLA_CHUNK = 64
GLA_SUB = 8
GLA_LEVELS = 3
NA_HEADS = 8
NA_KH = 8
NA_KW = 16
NA_ROWS_PER_STEP = 4
N_GROUPS = 4
EXPERTS_PER_GROUP = 8
N_EXPERTS = N_GROUPS * EXPERTS_PER_GROUP
TOP_K = 2
LN_EPS = 1e-6
RMS_EPS = 1e-6
NEG_BIG = -1e30
LOG2E = 1.4426950408889634

LANES = 128
TM = 256
MOE_BLK = 256
DMA_UNROLL = 8
VMEM_LIMIT = 56 * 1024 * 1024

A_Q, A_KV = ATTN_HEADS * HEAD_DIM, ATTN_KV_HEADS * HEAD_DIM
B_QK, B_V, B_A = GLA_HEADS * GLA_DK, GLA_HEADS * GLA_DV, 2 * GLA_GATE_RANK
C_W = NA_HEADS * HEAD_DIM
NCOL_A = A_Q + 2 * A_KV
ZA_COLS = A_Q + 3 * ATTN_KV_HEADS * LANES
NCOL_B = 2 * B_QK + 2 * B_V + LANES
NCOL_C = 3 * C_W


def _cparams(sem):
    return pltpu.CompilerParams(dimension_semantics=sem, vmem_limit_bytes=VMEM_LIMIT)


def _ln0(x):
    mu = jnp.mean(x, axis=-1, keepdims=True)
    xc = x - mu
    var = jnp.mean(xc * xc, axis=-1, keepdims=True)
    return xc * lax.rsqrt(var + LN_EPS)


def _modulate(x, shift, scale):
    return _ln0(x) * (1.0 + scale) + shift


def _nt(a, b):
    return lax.dot_general(a, b, (((1,), (1,)), ((), ())), preferred_element_type=F32)


def _dot(a, b, precision=None):
    return jnp.dot(a, b, preferred_element_type=F32, precision=precision)


def _mod_kernel(c_ref, w_ref, b_ref, o_ref):
    c = c_ref[...]
    s = c * jax.nn.sigmoid(c)
    o_ref[0] = _dot(s, w_ref[0]) + b_ref[0]


def _mod_call(cin, w_mod, b_mod):
    depth, d, nmod = w_mod.shape
    nb = nmod // d
    return pl.pallas_call(
        _mod_kernel,
        grid=(depth, nb),
        in_specs=[pl.BlockSpec((8, d), lambda l, n: (0, 0)),
                  pl.BlockSpec((1, d, d), lambda l, n: (l, 0, n)),
                  pl.BlockSpec((1, 1, d), lambda l, n: (l, 0, n))],
        out_specs=pl.BlockSpec((1, 8, d), lambda l, n: (l, 0, n)),
        out_shape=jax.ShapeDtypeStruct((depth, 8, nmod), F32),
        compiler_params=_cparams(("arbitrary", "arbitrary")),
        name="adaln_mod",
    )(cin, w_mod, b_mod.reshape(depth, 1, nmod))


def _proj_kernel(x_ref, mod_ref, w_ref, b_ref, cos_ref, sa_ref, sb_ref, qn_ref, kn_ref, gm_ref,
                 za_ref, zb_ref, zc_ref):
    h = _modulate(x_ref[...], mod_ref[0, 0:1, :], mod_ref[0, 1:2, :]).astype(BF16)
    lo = NCOL_A
    for o_ref in (zb_ref, zc_ref):
        hi = lo + o_ref.shape[1]
        acc = _dot(h, w_ref[:, lo:hi]) + b_ref[:, lo:hi]
        o_ref[...] = acc.astype(o_ref.dtype)
        lo = hi

    gm = gm_ref[...]
    lo_half = lax.broadcasted_iota(jnp.int32, (1, LANES), 1) < HEAD_DIM

    def rms(x, g):
        x2 = x * x
        hi2 = x2.astype(BF16)
        lo2 = (x2 - hi2.astype(F32)).astype(BF16)
        ms = _dot(hi2, gm) + _dot(lo2, gm)
        return x * lax.rsqrt(ms + RMS_EPS) * g

    def rope(x):
        q4 = HEAD_DIM // 4
        return x * cos_ref[...] + pltpu.roll(x, q4, 1) * sa_ref[...] + pltpu.roll(x, LANES - q4, 1) * sb_ref[...]

    def both_halves(x):
        sw = pltpu.roll(x, HEAD_DIM, 1)
        return [jnp.where(lo_half, x, sw), jnp.where(lo_half, sw, x)]

    acc = _dot(h, w_ref[:, 0:NCOL_A]) + b_ref[:, 0:NCOL_A]
    parts = []
    for i in range(A_Q // LANES):
        parts.append(rope(rms(acc[:, i * LANES:(i + 1) * LANES], qn_ref[...])) * (HEAD_DIM ** -0.5 * LOG2E))
    parts += both_halves(rope(rms(acc[:, A_Q:A_Q + A_KV], kn_ref[...])))
    for vv in both_halves(acc[:, A_Q + A_KV:A_Q + 2 * A_KV]):
        parts += [jnp.where(lo_half, vv, 1.0), jnp.where(lo_half, 1.0, vv)]
    za_ref[...] = jnp.concatenate(parts, axis=1).astype(za_ref.dtype)


def _mod_index(nt_b, n_tiles):
    ctx_row = n_tiles // nt_b
    return lambda j: (jnp.where(j % nt_b == 0, ctx_row, j // nt_b), 0, 0)


def _proj_call(xs, mods, w_p, b_p, tabs, q_norm, k_norm, gm, nt_b):
    t, d = xs.shape
    ncol = w_p.shape[1]
    tab = pl.BlockSpec((TM, LANES), lambda j: (j % nt_b, 0))
    vec = pl.BlockSpec((1, LANES), lambda j: (0, 0))
    return pl.pallas_call(
        _proj_kernel,
        grid=(t // TM,),
        in_specs=[pl.BlockSpec((TM, d), lambda j: (j, 0)),
                  pl.BlockSpec((1, 6, d), _mod_index(nt_b, t // TM)),
                  pl.BlockSpec((d, ncol), lambda j: (0, 0)),
                  pl.BlockSpec((1, ncol), lambda j: (0, 0)),
                  tab, tab, tab, vec, vec,
                  pl.BlockSpec((LANES, LANES), lambda j: (0, 0))],
        out_specs=[pl.BlockSpec((TM, ZA_COLS), lambda j: (j, 0)),
                   pl.BlockSpec((TM, NCOL_B), lambda j: (j, 0)),
                   pl.BlockSpec((TM, NCOL_C), lambda j: (j, 0))],
        out_shape=[jax.ShapeDtypeStruct((t, ZA_COLS), BF16),
                   jax.ShapeDtypeStruct((t, NCOL_B), F32),
                   jax.ShapeDtypeStruct((t, NCOL_C), BF16)],
        compiler_params=_cparams(("arbitrary",)),
        name="in_proj",
    )(xs, mods, w_p, b_p, *tabs, q_norm, k_norm, gm)


def _gqa_kernel(q_ref, k_ref, v_ref, o_ref, *, n_keys, ctx_len):
    qi = pl.program_id(2)
    lo_half = lax.broadcasted_iota(jnp.int32, (1, LANES), 1) < HEAD_DIM
    group = ATTN_HEADS // ATTN_KV_HEADS
    zero = jnp.zeros((TM, LANES), BF16)

    def attend(nk):
        outs = []
        for g in range(group):
            xq = q_ref[:, (g // 2) * LANES:(g // 2 + 1) * LANES]
            qm = jnp.where(lo_half if g % 2 == 0 else jnp.logical_not(lo_half), xq, zero)
            s = _nt(qm, k_ref[0:nk, :])
            p = jnp.exp2(s - jnp.max(s, axis=-1, keepdims=True)).astype(BF16)
            outs.append(_dot(p, v_ref[0:nk, (g % 2) * LANES:(g % 2 + 1) * LANES]))
        tiles = []
        for i in range(group // 2):
            num = jnp.where(lo_half, outs[2 * i], outs[2 * i + 1])
            den = pltpu.roll(jnp.where(lo_half, outs[2 * i + 1], outs[2 * i]), HEAD_DIM, 1)
            tiles.append(num / den)
        o_ref[...] = jnp.concatenate(tiles, axis=1).astype(o_ref.dtype)

    pl.when(qi == 0)(lambda: attend(ctx_len))
    pl.when(qi > 0)(lambda: attend(n_keys))


def _gqa_call(za, bsz, tb):
    t = za.shape[0]
    nt_b = tb // TM
    k_blk = A_Q // LANES
    return pl.pallas_call(
        functools.partial(_gqa_kernel, n_keys=tb, ctx_len=TM),
        grid=(bsz, ATTN_KV_HEADS, nt_b),
        in_specs=[pl.BlockSpec((TM, 2 * LANES), lambda b, h, i: (b * nt_b + i, h)),
                  pl.BlockSpec((tb, LANES), lambda b, h, i: (b, k_blk + h)),
                  pl.BlockSpec((tb, 2 * LANES), lambda b, h, i: (b, (k_blk + ATTN_KV_HEADS) // 2 + h))],
        out_specs=pl.BlockSpec((TM, 2 * LANES), lambda b, h, i: (b * nt_b + i, h)),
        out_shape=jax.ShapeDtypeStruct((t, A_Q), BF16),
        compiler_params=_cparams(("arbitrary", "arbitrary", "arbitrary")),
        name="gqa_axial",
    )(za, za, za)


def _gla_kernel(*refs, reverse):
    if reverse:
        (q_ref, k_ref, v_ref, a_ref, r_ref, of_ref, wg_ref, bg_ref, mat_ref, hsel_ref, lvl_ref, vmask_ref,
         ng_ref, o_ref, st_s) = refs
    else:
        q_ref, k_ref, v_ref, a_ref, wg_ref, bg_ref, mat_ref, hsel_ref, lvl_ref, vmask_ref, o_ref, st_s = refs
    C, SUB, H = GLA_CHUNK, GLA_SUB, GLA_HEADS
    rows_t = q_ref.shape[0]

    @pl.when(pl.program_id(1) == 0)
    def _zero_state():
        st_s[...] = jnp.zeros_like(st_s)

    lane_k = lax.broadcasted_iota(jnp.int32, (1, B_QK), 1) // GLA_DK
    hmask = [lane_k == h for h in range(H)]
    lane_s = lax.broadcasted_iota(jnp.int32, (1, LANES), 1) // SUB

    def split3(x):
        x1 = x.astype(BF16)
        r1 = x - x1.astype(F32)
        x2 = r1.astype(BF16)
        return [x1, x2, (r1 - x2.astype(F32)).astype(BF16)]

    def stack_heads(x):
        return jnp.concatenate([jnp.where(hmask[h], x, 0.0) for h in range(H)], axis=0).astype(BF16)

    def head_diag(y, r, w):
        return jnp.concatenate([y[h * r:(h + 1) * r, h * w:(h + 1) * w] for h in range(H)], axis=1)

    q = q_ref[...] * (GLA_DK ** -0.5)
    k = k_ref[...]
    vb = v_ref[...].astype(BF16)
    a = a_ref[...]
    a_hi = a.astype(BF16)
    a_lo = (a - a_hi.astype(F32)).astype(BF16)
    xw = _dot(a_hi, wg_ref[...])
    x = xw[:, 0:B_QK] + xw[:, B_QK:2 * B_QK] + _dot(a_lo, wg_ref[:, 0:B_QK]) + bg_ref[...]
    g = (jnp.minimum(x, 0.0) - jnp.log(1.0 + jnp.exp(-jnp.abs(x)))) * (1.0 / GLA_TAU)
    cm = _dot(mat_ref[...], jnp.concatenate(split3(g), axis=1))
    b = cm[:, 0:B_QK] + cm[:, B_QK:2 * B_QK] + cm[:, 2 * B_QK:3 * B_QK]

    def group_rows(grp, row_in_group):
        return jnp.concatenate([jnp.broadcast_to(b[r0 + row_in_group:r0 + row_in_group + 1, :], (grp, B_QK))
                                for r0 in range(0, rows_t, grp)], axis=0)

    to_end = group_rows(C, 0 if reverse else C - 1) - b
    qb = q * jnp.exp(b)
    kend = (k * jnp.exp(to_end)).astype(BF16)
    q_lv, k_lv = [], []
    for lv in range(GLA_LEVELS):
        grp = C >> lv
        ref = group_rows(grp, grp // 2 if reverse else grp // 2 - 1)
        e_lv = jnp.exp(-jnp.abs(b - ref))
        q_lv.append(q * e_lv)
        k_lv.append((k * e_lv).astype(BF16))
    terms = []
    for d in range(SUB):
        sh = (rows_t - d) % rows_t if reverse else d
        kd = pltpu.roll(k, sh, 0) if sh else k
        bd = pltpu.roll(b, sh, 0) if sh else b
        terms.append((q * kd * jnp.exp(jnp.minimum(b - bd, 0.0))).astype(BF16))
    sacc = jnp.where(vmask_ref[...] != 0.0, _dot(jnp.concatenate(terms, axis=1), hsel_ref[...]), 0.0)
    level = lvl_ref[...]

    st = st_s[...]
    n_chunks = rows_t // C
    order = range(n_chunks - 1, -1, -1) if reverse else range(n_chunks)
    for c in order:
        rows = slice(c * C, (c + 1) * C)
        diag = []
        for h in range(H):
            xh = jnp.where(lane_s == h, sacc[rows, :], 0.0)
            sh = (LANES - h * SUB - (0 if reverse else SUB - 1)) % LANES
            diag.append(pltpu.roll(xh, sh, 1, stride=1, stride_axis=0))
        p = jnp.concatenate(diag, axis=0)[:, 0:C]
        for lv in range(GLA_LEVELS):
            p = jnp.where(level == lv + 1, _nt(stack_heads(q_lv[lv][rows, :]), k_lv[lv][rows, :]), p)
        o = head_diag(_dot(p.astype(BF16), vb[rows, :]), C, GLA_DV)

        o_int = _nt(stack_heads(qb[rows, :]), st.astype(BF16))
        o = o + jnp.concatenate([o_int[h * C:(h + 1) * C, :] for h in range(H)], axis=1)
        upd = lax.dot_general(vb[rows, :], kend[rows, :], (((0,), (0,)), ((), ())),
                              preferred_element_type=F32)
        new = upd[(H - 1) * GLA_DV:H * GLA_DV, :]
        for h in range(H - 2, -1, -1):
            new = jnp.where(hmask[h], upd[h * GLA_DV:(h + 1) * GLA_DV, :], new)
        r_end = c * C if reverse else (c + 1) * C - 1
        st = st * jnp.exp(b[r_end:r_end + 1, :]) + new

        if reverse:
            ot = o + of_ref[rows, :]
            outs = []
            for h in range(H):
                oh = ot[:, h * GLA_DV:(h + 1) * GLA_DV]
                ms = jnp.mean(oh * oh, axis=-1, keepdims=True)
                outs.append(oh * lax.rsqrt(ms + RMS_EPS) * ng_ref[...])
            r = r_ref[rows, :]
            o_ref[rows, :] = (jnp.concatenate(outs, axis=1) * (r * jax.nn.sigmoid(r))).astype(o_ref.dtype)
        else:
            o_ref[rows, :] = o
    st_s[...] = st


def _gla_consts(reverse):
    C, S, H = GLA_CHUNK, GLA_SUB, GLA_HEADS
    assert C == S * 2 ** GLA_LEVELS and H * S <= LANES
    t = np.arange(TM)
    chunk, tt = t // C, t % C
    tau = (C - 1 - tt) if reverse else tt
    same = chunk[:, None] == chunk[None, :]
    incl = (same & (tau[None, :] <= tau[:, None])).astype(np.float32)
    level = np.zeros((C, C), np.int32)
    for lv in range(GLA_LEVELS):
        grp = C >> lv
        later = (tau % grp) >= grp // 2
        tl, ll = tau[:C], later[:C]
        pair = (tl[:, None] // grp == tl[None, :] // grp) & ll[:, None] & ~ll[None, :]
        level[pair] = lv + 1
    hsel = np.zeros((S, B_QK, LANES), np.float32)
    for d in range(S):
        j = d if reverse else S - 1 - d
        for h in range(H):
            hsel[d, h * GLA_DK:(h + 1) * GLA_DK, h * S + j] = 1.0
    lane = np.arange(LANES)
    d_of_lane = (lane % S) if reverse else (S - 1 - lane % S)
    in_blk = ((tt % S)[:, None] + d_of_lane[None, :] <= S - 1) if reverse else ((tt % S)[:, None] >= d_of_lane[None, :])
    vmask = ((lane < H * S)[None, :] & in_blk).astype(np.float32)
    return (jnp.asarray(incl, BF16), jnp.asarray(hsel.reshape(S * B_QK, LANES), BF16),
            jnp.asarray(np.tile(level, (H, 1))), jnp.asarray(vmask))


def _gla_call(zb, o_fwd, wg, bg, ng, bsz, tb, reverse):
    t = zb.shape[0]
    nt_b = tb // TM
    consts = _gla_consts(reverse)
    const_specs = [pl.BlockSpec(c.shape, lambda b, i: (0, 0)) for c in consts]
    if reverse:
        def blk(b, i):
            return b * nt_b + jnp.where(i == 0, 0, nt_b - i)
    else:
        def blk(b, i):
            return b * nt_b + i
    qs = pl.BlockSpec((TM, B_QK), lambda b, i: (blk(b, i), 0))
    ks = pl.BlockSpec((TM, B_QK), lambda b, i: (blk(b, i), 1))
    vs = pl.BlockSpec((TM, B_V), lambda b, i: (blk(b, i), 1))
    rsp = pl.BlockSpec((TM, B_V), lambda b, i: (blk(b, i), 2))
    asp = pl.BlockSpec((TM, LANES), lambda b, i: (blk(b, i), (2 * B_QK + 2 * B_V) // LANES))
    osp = pl.BlockSpec((TM, B_V), lambda b, i: (blk(b, i), 0))
    wsp = pl.BlockSpec((LANES, 2 * B_QK), lambda b, i: (0, 0))
    bsp = pl.BlockSpec((1, B_QK), lambda b, i: (0, 0))
    if reverse:
        in_specs = [qs, ks, vs, asp, rsp, osp, wsp, bsp] + const_specs + [
            pl.BlockSpec((1, GLA_DV), lambda b, i: (0, 0))]
        args = (zb, zb, zb, zb, zb, o_fwd, wg, bg) + consts + (ng,)
        out_dtype = BF16
    else:
        in_specs = [qs, ks, vs, asp, wsp, bsp] + const_specs
        args = (zb, zb, zb, zb, wg, bg) + consts
        out_dtype = F32
    return pl.pallas_call(
        functools.partial(_gla_kernel, reverse=reverse),
        grid=(bsz, nt_b),
        in_specs=in_specs,
        out_specs=osp,
        out_shape=jax.ShapeDtypeStruct((t, B_V), out_dtype),
        scratch_shapes=[pltpu.VMEM((GLA_DV, B_QK), F32)],
        compiler_params=_cparams(("arbitrary", "arbitrary")),
        name="gla_bwd" if reverse else "gla_fwd",
    )(*args)


def _na_kernel(q_ref, k_ref, v_ref, bias_ref, o_ref, *, n_rows, ctx_len):
    i = pl.program_id(1)
    W = GRID_W
    rows_per_step = q_ref.shape[0] // W
    n_ctx_steps = ctx_len // q_ref.shape[0]
    lo_half = lax.broadcasted_iota(jnp.int32, (1, LANES), 1) < HEAD_DIM
    scale = HEAD_DIM ** -0.5
    n_pairs = C_W // LANES

    def pair_queries(q0, nq, pr):
        qp = q_ref[q0:q0 + nq, pr * LANES:(pr + 1) * LANES]
        zero = jnp.zeros_like(qp)
        return jnp.concatenate([jnp.where(lo_half, qp, zero), jnp.where(lo_half, zero, qp)], axis=0)

    def pair_out(o, nq):
        return jnp.where(lo_half, o[0:nq, :], o[nq:2 * nq, :])

    @pl.when(i < n_ctx_steps)
    def _ctx_queries():
        nq = q_ref.shape[0]
        outs = []
        for pr in range(n_pairs):
            cols = slice(pr * LANES, (pr + 1) * LANES)
            s_c = _nt(pair_queries(0, nq, pr), k_ref[0:ctx_len, cols]) * scale
            m = jnp.max(s_c, axis=-1, keepdims=True)
            p = jnp.exp(s_c - m)
            l = jnp.sum(p, axis=-1, keepdims=True)
            outs.append(pair_out(_dot(p.astype(BF16), v_ref[0:ctx_len, cols]) / l, nq))
        o_ref[...] = jnp.concatenate(outs, axis=1).astype(o_ref.dtype)

    @pl.when(i >= n_ctx_steps)
    def _grid_rows():
        for sub in range(rows_per_step):
            r = (i - n_ctx_steps) * rows_per_step + sub
            rs = jnp.clip(r - NA_KH // 2, 0, n_rows - NA_KH)
            start = pl.multiple_of(ctx_len + rs * W, W)
            win = pl.ds(start, NA_KH * W)
            outs = []
            for pr in range(n_pairs):
                cols = slice(pr * LANES, (pr + 1) * LANES)
                qs = pair_queries(sub * W, W, pr)
                s_c = _nt(qs, k_ref[0:ctx_len, cols]) * scale
                s_w = _nt(qs, k_ref[win, cols]) * scale + bias_ref[r - rs, pr * 2 * W:(pr + 1) * 2 * W, :]
                m = jnp.maximum(jnp.max(s_w, axis=-1, keepdims=True), jnp.max(s_c, axis=-1, keepdims=True))
                p_w = jnp.exp(s_w - m)
                p_c = jnp.exp(s_c - m)
                l = jnp.sum(p_w, axis=-1, keepdims=True) + jnp.sum(p_c, axis=-1, keepdims=True)
                o = _dot(p_w.astype(BF16), v_ref[win, cols]) + _dot(p_c.astype(BF16), v_ref[0:ctx_len, cols])
                outs.append(pair_out(o / l, W))
            o_ref[sub * W:(sub + 1) * W, :] = jnp.concatenate(outs, axis=1).astype(o_ref.dtype)


def _na_call(zc, bias_d, bsz, tb):
    t = zc.shape[0]
    q_rows = NA_ROWS_PER_STEP * GRID_W
    steps = tb // q_rows
    n_rows = (tb - TM) // GRID_W
    assert TM % q_rows == 0 and n_rows % NA_ROWS_PER_STEP == 0
    kern = functools.partial(_na_kernel, n_rows=n_rows, ctx_len=TM)
    return pl.pallas_call(
        kern,
        grid=(bsz, steps),
        in_specs=[pl.BlockSpec((q_rows, C_W), lambda b, i: (b * steps + i, 0)),
                  pl.BlockSpec((tb, C_W), lambda b, i: (b, 1)),
                  pl.BlockSpec((tb, C_W), lambda b, i: (b, 2)),
                  pl.BlockSpec(bias_d.shape, lambda b, i: (0, 0, 0))],
        out_specs=pl.BlockSpec((q_rows, C_W), lambda b, i: (b * steps + i, 0)),
        out_shape=jax.ShapeDtypeStruct((t, C_W), BF16),
        compiler_params=_cparams(("arbitrary", "arbitrary")),
        name="nbr_attn",
    )(zc, zc, zc, bias_d)


def _merge_kernel(x_ref, mod_ref, oa_ref, ob_ref, oc_ref, wg_ref, bg_ref, wa_ref, wb_ref, wc_ref,
                  wo_ref, lg_ref, lb_ref, wr_ref, br_ref, x1_ref, h2_ref, rt_ref, *, alpha):
    d = x_ref.shape[1]
    x = x_ref[...]
    h = _modulate(x, mod_ref[0, 0:1, :], mod_ref[0, 1:2, :]).astype(BF16)
    y = jnp.zeros_like(x)
    for bi, (o_ref, w_ref) in enumerate(((oa_ref, wa_ref), (ob_ref, wb_ref), (oc_ref, wc_ref))):
        gate = jax.nn.sigmoid(_dot(h, wg_ref[:, bi * d:(bi + 1) * d]) + bg_ref[:, bi * d:(bi + 1) * d])
        y = y + gate * _dot(o_ref[...], w_ref[...])
    y2 = _dot(y.astype(BF16), wo_ref[...])
    x1 = _ln0(alpha * x + mod_ref[0, 2:3, :] * y2) * lg_ref[...] + lb_ref[...]
    x1_ref[...] = x1
    h2 = _modulate(x1, mod_ref[0, 3:4, :], mod_ref[0, 4:5, :])
    h2_ref[...] = h2

    h2_hi = h2.astype(BF16)
    h2_lo = (h2 - h2_hi.astype(F32)).astype(BF16)
    hw = _dot(h2_hi, wr_ref[...])
    logits = hw[:, 0:LANES] + hw[:, LANES:2 * LANES] + _dot(h2_lo, wr_ref[:, 0:LANES]) + br_ref[...]
    lane = lax.broadcasted_iota(jnp.int32, logits.shape, 1)
    big = jnp.int32(1 << 20)
    neg = jnp.float32(-jnp.inf)

    def amax(vals):
        m = jnp.max(vals, axis=-1, keepdims=True)
        return m, jnp.min(jnp.where(vals == m, lane, big), axis=-1, keepdims=True)

    gl = jnp.where(lane < N_GROUPS, logits, neg)
    gmax, grp = amax(gl)
    p_grp = 1.0 / jnp.sum(jnp.exp(gl - gmax), axis=-1, keepdims=True)
    lo = N_GROUPS + grp * EXPERTS_PER_GROUP
    sl = jnp.where((lane >= lo) & (lane < lo + EXPERTS_PER_GROUP), logits, neg)
    v1, i1 = amax(sl)
    v2, i2 = amax(jnp.where(lane == i1, neg, sl))
    e2 = jnp.exp(v2 - v1)
    g1 = p_grp / (1.0 + e2)
    g2 = p_grp * e2 / (1.0 + e2)
    rt = jnp.where(lane == 0, (i1 - N_GROUPS).astype(F32),
                   jnp.where(lane == 1, (i2 - N_GROUPS).astype(F32),
                             jnp.where(lane == 2, g1, jnp.where(lane == 3, g2, 0.0))))
    rt_ref[...] = rt


def _merge_call(xs, mods, oa, ob, oc, wgate, bgate, wa, wb, wc, wo, lg, lb, wr, br, nt_b, alpha):
    t, d = xs.shape
    full = lambda shape: pl.BlockSpec(shape, lambda j: tuple(0 for _ in shape))
    row = lambda w: pl.BlockSpec((TM, w), lambda j: (j, 0))
    return pl.pallas_call(
        functools.partial(_merge_kernel, alpha=alpha),
        grid=(t // TM,),
        in_specs=[row(d), pl.BlockSpec((1, 6, d), _mod_index(nt_b, t // TM)), row(A_Q), row(B_V), row(C_W),
                  full(wgate.shape), full(bgate.shape), full(wa.shape), full(wb.shape), full(wc.shape),
                  full(wo.shape), full(lg.shape), full(lb.shape), full(wr.shape), full(br.shape)],
        out_specs=[row(d), row(d), row(LANES)],
        out_shape=[jax.ShapeDtypeStruct((t, d), F32), jax.ShapeDtypeStruct((t, d), F32),
                   jax.ShapeDtypeStruct((t, LANES), F32)],
        compiler_params=_cparams(("arbitrary",)),
        name="merge_router",
    )(xs, mods, oa, ob, oc, wgate, bgate, wa, wb, wc, wo, lg, lb, wr, br)


def _dispatch_kernel(pos_ref, h_ref, xs_in_ref, xs_ref, sem):
    del xs_in_ref

    def row_copy(tok, slot):
        return pltpu.make_async_copy(h_ref.at[pl.ds(tok, 1)], xs_ref.at[pl.ds(slot, 1)], sem)

    def issue(tok, carry):
        for kk in range(TOP_K):
            row_copy(tok, pos_ref[0, 0, TOP_K * tok + kk]).start()
        return carry

    lax.fori_loop(0, TM, issue, 0, unroll=DMA_UNROLL)
    for kk in range(TOP_K):
        pltpu.make_async_copy(h_ref, xs_ref.at[pl.ds(0, TM)], sem).wait()


def _dispatch_call(pos3, h2, n_slots):
    t, d = h2.shape
    zeros = jnp.zeros((n_slots, d), F32)
    return pl.pallas_call(
        _dispatch_kernel,
        grid=(t // TM,),
        in_specs=[pl.BlockSpec((1, 1, TOP_K * TM), lambda j: (j, 0, 0), memory_space=pltpu.SMEM),
                  pl.BlockSpec((TM, d), lambda j: (j, 0)),
                  pl.BlockSpec(memory_space=pl.ANY)],
        out_specs=pl.BlockSpec(memory_space=pl.ANY),
        out_shape=jax.ShapeDtypeStruct((n_slots, d), F32),
        scratch_shapes=[pltpu.SemaphoreType.DMA(())],
        input_output_aliases={2: 0},
        compiler_params=_cparams(("arbitrary",)),
        name="moe_dispatch",
    )(pos3, h2, zeros)


def _expert_kernel(be_ref, x_ref, w1_ref, w3_ref, w2_ref, y_ref):
    del be_ref
    xb = x_ref[...].astype(BF16)
    a = _dot(xb, w1_ref[0].astype(BF16))
    b = _dot(xb, w3_ref[0].astype(BF16))
    mid = (a * jax.nn.sigmoid(a) * b).astype(BF16)
    y_ref[...] = _dot(mid, w2_ref[0].astype(BF16))


def _expert_call(block_expert, x_slots, w1, w3, w2):
    n_slots, d = x_slots.shape
    de = w1.shape[2]
    grid_spec = pltpu.PrefetchScalarGridSpec(
        num_scalar_prefetch=1,
        grid=(n_slots // MOE_BLK,),
        in_specs=[pl.BlockSpec((MOE_BLK, d), lambda j, be: (j, 0)),
                  pl.BlockSpec((1, d, de), lambda j, be: (be[j], 0, 0)),
                  pl.BlockSpec((1, d, de), lambda j, be: (be[j], 0, 0)),
                  pl.BlockSpec((1, de, d), lambda j, be: (be[j], 0, 0))],
        out_specs=pl.BlockSpec((MOE_BLK, d), lambda j, be: (j, 0)),
    )
    return pl.pallas_call(
        _expert_kernel,
        grid_spec=grid_spec,
        out_shape=jax.ShapeDtypeStruct((n_slots, d), F32),
        compiler_params=_cparams(("arbitrary",)),
        name="moe_experts",
    )(block_expert, x_slots, w1, w3, w2)


def _combine_kernel(pos_ref, x1_ref, mod_ref, rt_ref, ys_ref, lg_ref, lb_ref, o_ref, buf, sem, *, alpha):
    def row_copy(slot, kk, tok):
        return pltpu.make_async_copy(ys_ref.at[pl.ds(slot, 1)], buf.at[kk, pl.ds(tok, 1)], sem)

    def issue(tok, carry):
        for kk in range(TOP_K):
            row_copy(pos_ref[0, 0, TOP_K * tok + kk], kk, tok).start()
        return carry

    lax.fori_loop(0, TM, issue, 0, unroll=DMA_UNROLL)
    for kk in range(TOP_K):
        pltpu.make_async_copy(ys_ref.at[pl.ds(0, TM)], buf.at[kk], sem).wait()
    rt = rt_ref[...]
    y = rt[:, 2:3] * buf[0] + rt[:, 3:4] * buf[1]
    x2 = _ln0(alpha * x1_ref[...] + mod_ref[0, 5:6, :] * y) * lg_ref[...] + lb_ref[...]
    o_ref[...] = x2


def _combine_call(pos3, x1, mods, rt, y_slots, lg, lb, nt_b, alpha, latent_only):
    t, d = x1.shape
    vec = pl.BlockSpec((1, d), lambda j: (0, 0))
    if latent_only:
        n_lat = nt_b - 1
        n_out = (t // TM // nt_b) * n_lat
        tile = lambda j: (j // n_lat) * nt_b + 1 + j % n_lat
        mod_idx = lambda j: (j // n_lat, 0, 0)
    else:
        n_out = t // TM
        tile = lambda j: j
        mod_idx = _mod_index(nt_b, t // TM)
    return pl.pallas_call(
        functools.partial(_combine_kernel, alpha=alpha),
        grid=(n_out,),
        in_specs=[pl.BlockSpec((1, 1, TOP_K * TM), lambda j: (tile(j), 0, 0), memory_space=pltpu.SMEM),
                  pl.BlockSpec((TM, d), lambda j: (tile(j), 0)),
                  pl.BlockSpec((1, 6, d), mod_idx),
                  pl.BlockSpec((TM, LANES), lambda j: (tile(j), 0)),
                  pl.BlockSpec(memory_space=pl.ANY), vec, vec],
        out_specs=pl.BlockSpec((TM, d), lambda j: (j, 0)),
        out_shape=jax.ShapeDtypeStruct((n_out * TM, d), F32),
        scratch_shapes=[pltpu.VMEM((TOP_K, TM, d), F32), pltpu.SemaphoreType.DMA(())],
        compiler_params=_cparams(("arbitrary",)),
        name="moe_combine",
    )(pos3, x1, mods, rt, y_slots, lg, lb)


def _slot_plan(rt, n_blocks):
    flat_e = rt[:, 0:TOP_K].astype(jnp.int32).reshape(-1)
    onehot = (flat_e[:, None] == jnp.arange(N_EXPERTS, dtype=jnp.int32)[None, :]).astype(jnp.int32)
    csum = jnp.cumsum(onehot, axis=0)
    counts = csum[-1]
    padded = ((counts + MOE_BLK - 1) // MOE_BLK) * MOE_BLK
    pad_end = jnp.cumsum(padded)
    pad_start = pad_end - padded
    pos = jnp.sum(onehot * (csum - 1 + pad_start[None, :]), axis=1)
    block_start = jnp.arange(n_blocks, dtype=jnp.int32) * MOE_BLK
    block_expert = jnp.minimum(
        jnp.sum((pad_end[None, :] <= block_start[:, None]).astype(jnp.int32), axis=1), N_EXPERTS - 1)
    return pos.astype(jnp.int32), block_expert.astype(jnp.int32)


def _rope_tables(seq_len, ctx_len):
    n_freq = HEAD_DIM // 4
    t = jnp.arange(seq_len)
    invf = ROPE_THETA ** (-jnp.arange(n_freq, dtype=F32) / n_freq)
    ar = (t // GRID_W).astype(F32)[:, None] * invf
    ac = (t % GRID_W).astype(F32)[:, None] * invf
    ang = jnp.concatenate([ar, ar, ac, ac], axis=-1)
    cos, sin = jnp.cos(ang), jnp.sin(ang)
    quarter = (np.arange(HEAD_DIM) // n_freq) % 2
    sa = sin * jnp.asarray(quarter == 1, F32)
    sb = -sin * jnp.asarray(quarter == 0, F32)
    ones = jnp.ones((ctx_len, HEAD_DIM), F32)
    zeros = jnp.zeros((ctx_len, HEAD_DIM), F32)
    tabs = []
    for lat, ctx in ((cos, ones), (sa, zeros), (sb, zeros)):
        tab = jnp.concatenate([ctx, lat], axis=0)
        tabs.append(jnp.tile(tab, (1, LANES // HEAD_DIM)))
    return tabs


def _na_bias_table(rpb):
    kh = NA_KH
    w = np.arange(GRID_W)
    col_start = np.clip(w - NA_KW // 2, 0, GRID_W - NA_KW)
    colk = np.arange(GRID_W)
    inside = (colk[None, :] >= col_start[:, None]) & (colk[None, :] < col_start[:, None] + NA_KW)
    col_rel = colk[None, :] - w[:, None] + (NA_KW - 1)
    pats = np.arange(kh)
    row_rel = np.arange(kh)[None, :] - pats[:, None] + (NA_KH - 1)
    row_sel = (row_rel[:, :, None] == np.arange(2 * NA_KH - 1)).astype(np.float32)
    col_sel = ((col_rel[:, :, None] == np.arange(2 * NA_KW - 1)) & inside[:, :, None]).astype(np.float32)
    b = jnp.einsum('lhab,pia,wcb->lphwic', rpb, jnp.asarray(row_sel), jnp.asarray(col_sel), precision=HIGHEST)
    b = jnp.where(jnp.asarray(inside)[None, None, None, :, None, :], b, NEG_BIG)
    return b.reshape(rpb.shape[0], kh, NA_HEADS * GRID_W, kh * GRID_W).astype(F32)


def _pack_in_proj(w_in, b_in):
    a0 = 0
    a1 = a0 + NCOL_A
    bq0 = a1
    b_end = bq0 + 2 * B_QK + 2 * B_V + B_A
    c_end = b_end + NCOL_C
    d = w_in.shape[0]
    padw = jnp.zeros((d, LANES - B_A), w_in.dtype)
    padb = jnp.zeros((LANES - B_A,), b_in.dtype)
    w_p = jnp.concatenate([w_in[:, a0:a1], w_in[:, bq0:b_end], padw, w_in[:, b_end:c_end]], axis=1)
    b_p = jnp.concatenate([b_in[a0:a1], b_in[bq0:b_end], padb, b_in[b_end:c_end]])
    return w_p.astype(BF16), b_p.reshape(1, -1), w_in[:, c_end:].astype(BF16), b_in[c_end:].reshape(1, -1)


def kernel(x, c, ctx, c_ctx, w_mod, b_mod, w_in, b_in, attn_q_norm, attn_k_norm, gla_w_gate, gla_b_gate,
           gla_norm, na_rpb, w_br_attn, w_br_gla, w_br_na, w_out, ln1_g, ln1_b, w_router_group,
           b_router_group, w_router_expert, b_router_expert, moe_w1, moe_w3, moe_w2, ln2_g, ln2_b):
    bsz, seq, d = x.shape
    ctx_len = ctx.shape[1]
    depth = w_mod.shape[0]
    assert ctx_len == TM and seq % TM == 0 and (seq // GRID_W) >= NA_KH
    tb = ctx_len + seq
    nt_b = tb // TM
    t = bsz * tb
    alpha = (2 * depth) ** 0.25

    xs = jnp.concatenate([ctx, x], axis=1).reshape(t, d)

    cin = jnp.zeros((8, d), F32).at[0:bsz].set(c).at[bsz].set(c_ctx)
    mod_all = _mod_call(cin, w_mod, b_mod)

    tabs = _rope_tables(seq, ctx_len)
    gm = jnp.asarray(np.kron(np.eye(LANES // HEAD_DIM), np.ones((HEAD_DIM, HEAD_DIM))) / HEAD_DIM, BF16)
    n_assign = t * TOP_K
    n_blocks = -(-n_assign // MOE_BLK) + N_EXPERTS
    n_slots = n_blocks * MOE_BLK
    na_bias = _na_bias_table(na_rpb)
    w1_all = moe_w1.reshape((depth * N_EXPERTS,) + moe_w1.shape[2:])
    w3_all = moe_w3.reshape((depth * N_EXPERTS,) + moe_w3.shape[2:])
    w2_all = moe_w2.reshape((depth * N_EXPERTS,) + moe_w2.shape[2:])

    for l in range(depth):
        m = mod_all[l].reshape(8, 6, d)
        mods = m

        w_p, b_p, w_gate_cols, b_gate_cols = _pack_in_proj(w_in[l], b_in[l])
        qn = jnp.tile(attn_q_norm[l], LANES // HEAD_DIM).reshape(1, LANES)
        kn = jnp.tile(attn_k_norm[l], LANES // HEAD_DIM).reshape(1, LANES)
        za, zb, zc = _proj_call(xs, mods, w_p, b_p, tabs, qn, kn, gm, nt_b)
        oa = _gqa_call(za, bsz, tb)

        o_f = None
        for direction in range(2):
            wg = jnp.zeros((LANES, B_QK), F32).at[direction * GLA_GATE_RANK:(direction + 1) * GLA_GATE_RANK].set(
                gla_w_gate[l, direction])
            wg_hi = wg.astype(BF16)
            wg = jnp.concatenate([wg_hi, (wg - wg_hi.astype(F32)).astype(BF16)], axis=1)
            bg = gla_b_gate[l, direction].reshape(1, B_QK)
            o_f = _gla_call(zb, o_f, wg, bg, gla_norm[l].reshape(1, GLA_DV), bsz, tb, reverse=direction == 1)
        ob = o_f

        oc = _na_call(zc, na_bias[l], bsz, tb)

        wr = jnp.concatenate([w_router_group[l], w_router_expert[l],
                              jnp.zeros((d, LANES - N_GROUPS - N_EXPERTS), F32)], axis=1)
        br = jnp.concatenate([b_router_group[l], b_router_expert[l],
                              jnp.zeros((LANES - N_GROUPS - N_EXPERTS,), F32)]).reshape(1, LANES)
        wr_hi = wr.astype(BF16)
        wr = jnp.concatenate([wr_hi, (wr - wr_hi.astype(F32)).astype(BF16)], axis=1)
        x1, h2, rt = _merge_call(xs, mods, oa, ob, oc, w_gate_cols, b_gate_cols,
                                 w_br_attn[l].astype(BF16), w_br_gla[l].astype(BF16), w_br_na[l].astype(BF16),
                                 w_out[l].astype(BF16), ln1_g[l].reshape(1, d), ln1_b[l].reshape(1, d),
                                 wr, br, nt_b, alpha)

        pos, block_expert = _slot_plan(rt, n_blocks)
        pos3 = pos.reshape(t // TM, 1, TOP_K * TM)
        x_slots = _dispatch_call(pos3, h2, n_slots)
        y_slots = _expert_call(block_expert + l * N_EXPERTS, x_slots, w1_all, w3_all, w2_all)
        xs = _combine_call(pos3, x1, mods, rt, y_slots, ln2_g[l].reshape(1, d), ln2_b[l].reshape(1, d),
                           nt_b, alpha, latent_only=l == depth - 1)

    return xs.reshape(bsz, seq, d)
```

```python
import functools

import jax
import jax.numpy as jnp
import numpy as np
from jax import lax
from jax.experimental import pallas as pl
from jax.experimental.pallas import tpu as pltpu

F32 = jnp.float32
BF16 = jnp.bfloat16
HIGHEST = lax.Precision.HIGHEST

GRID_W = 64
HEAD_DIM = 64
ROPE_THETA = 10000.0
ATTN_HEADS = 8
ATTN_KV_HEADS = 2
GLA_HEADS = 4
GLA_DK = 64
GLA_DV = 128
GLA_GATE_RANK = 16
GLA_TAU = 16.0
GLA_CHUNK = 64
GLA_SUB = 8
GLA_LEVELS = 3
NA_HEADS = 8
NA_KH = 8
NA_KW = 16
NA_ROWS_PER_STEP = 4
N_GROUPS = 4
EXPERTS_PER_GROUP = 8
N_EXPERTS = N_GROUPS * EXPERTS_PER_GROUP
TOP_K = 2
LN_EPS = 1e-6
RMS_EPS = 1e-6
NEG_BIG = -1e30
LOG2E = 1.4426950408889634

LANES = 128
TM = 256
MOE_BLK = 256
DMA_UNROLL = 8
VMEM_LIMIT = 56 * 1024 * 1024

A_Q, A_KV = ATTN_HEADS * HEAD_DIM, ATTN_KV_HEADS * HEAD_DIM
B_QK, B_V, B_A = GLA_HEADS * GLA_DK, GLA_HEADS * GLA_DV, 2 * GLA_GATE_RANK
C_W = NA_HEADS * HEAD_DIM
NCOL_A = A_Q + 2 * A_KV
ZA_COLS = A_Q + 3 * ATTN_KV_HEADS * LANES
NCOL_B = 2 * B_QK + 2 * B_V + LANES
NCOL_C = 3 * C_W


def _cparams(sem):
    return pltpu.CompilerParams(dimension_semantics=sem, vmem_limit_bytes=VMEM_LIMIT)


def _ln0(x):
    mu = jnp.mean(x, axis=-1, keepdims=True)
    xc = x - mu
    var = jnp.mean(xc * xc, axis=-1, keepdims=True)
    return xc * lax.rsqrt(var + LN_EPS)


def _modulate(x, shift, scale):
    return _ln0(x) * (1.0 + scale) + shift


def _nt(a, b):
    return lax.dot_general(a, b, (((1,), (1,)), ((), ())), preferred_element_type=F32)


def _dot(a, b, precision=None):
    return jnp.dot(a, b, preferred_element_type=F32, precision=precision)


def _mod_kernel(c_ref, w_ref, b_ref, o_ref):
    c = c_ref[...]
    s = c * jax.nn.sigmoid(c)
    o_ref[0] = _dot(s, w_ref[0]) + b_ref[0]


def _mod_call(cin, w_mod, b_mod):
    depth, d, nmod = w_mod.shape
    nb = nmod // d
    return pl.pallas_call(
        _mod_kernel,
        grid=(depth, nb),
        in_specs=[pl.BlockSpec((8, d), lambda l, n: (0, 0)),
                  pl.BlockSpec((1, d, d), lambda l, n: (l, 0, n)),
                  pl.BlockSpec((1, 1, d), lambda l, n: (l, 0, n))],
        out_specs=pl.BlockSpec((1, 8, d), lambda l, n: (l, 0, n)),
        out_shape=jax.ShapeDtypeStruct((depth, 8, nmod), F32),
        compiler_params=_cparams(("arbitrary", "arbitrary")),
        name="adaln_mod",
    )(cin, w_mod, b_mod.reshape(depth, 1, nmod))


def _proj_kernel(x_ref, mod_ref, w_ref, b_ref, cos_ref, sa_ref, sb_ref, qn_ref, kn_ref, gm_ref,
                 za_ref, zb_ref, zc_ref):
    h = _modulate(x_ref[...], mod_ref[0, 0:1, :], mod_ref[0, 1:2, :]).astype(BF16)
    lo = NCOL_A
    for o_ref in (zb_ref, zc_ref):
        hi = lo + o_ref.shape[1]
        acc = _dot(h, w_ref[:, lo:hi]) + b_ref[:, lo:hi]
        o_ref[...] = acc.astype(o_ref.dtype)
        lo = hi

    gm = gm_ref[...]
    lo_half = lax.broadcasted_iota(jnp.int32, (1, LANES), 1) < HEAD_DIM

    def rms(x, g):
        x2 = x * x
        hi2 = x2.astype(BF16)
        lo2 = (x2 - hi2.astype(F32)).astype(BF16)
        ms = _dot(hi2, gm) + _dot(lo2, gm)
        return x * lax.rsqrt(ms + RMS_EPS) * g

    def rope(x):
        q4 = HEAD_DIM // 4
        return x * cos_ref[...] + pltpu.roll(x, q4, 1) * sa_ref[...] + pltpu.roll(x, LANES - q4, 1) * sb_ref[...]

    def both_halves(x):
        sw = pltpu.roll(x, HEAD_DIM, 1)
        return [jnp.where(lo_half, x, sw), jnp.where(lo_half, sw, x)]

    acc = _dot(h, w_ref[:, 0:NCOL_A]) + b_ref[:, 0:NCOL_A]
    parts = []
    for i in range(A_Q // LANES):
        parts.append(rope(rms(acc[:, i * LANES:(i + 1) * LANES], qn_ref[...])) * (HEAD_DIM ** -0.5 * LOG2E))
    parts += both_halves(rope(rms(acc[:, A_Q:A_Q + A_KV], kn_ref[...])))
    for vv in both_halves(acc[:, A_Q + A_KV:A_Q + 2 * A_KV]):
        parts += [jnp.where(lo_half, vv, 1.0), jnp.where(lo_half, 1.0, vv)]
    za_ref[...] = jnp.concatenate(parts, axis=1).astype(za_ref.dtype)


def _mod_index(nt_b, n_tiles):
    ctx_row = n_tiles // nt_b
    return lambda j: (jnp.where(j % nt_b == 0, ctx_row, j // nt_b), 0, 0)


def _proj_call(xs, mods, w_p, b_p, tabs, q_norm, k_norm, gm, nt_b):
    t, d = xs.shape
    ncol = w_p.shape[1]
    tab = pl.BlockSpec((TM, LANES), lambda j: (j % nt_b, 0))
    vec = pl.BlockSpec((1, LANES), lambda j: (0, 0))
    return pl.pallas_call(
        _proj_kernel,
        grid=(t // TM,),
        in_specs=[pl.BlockSpec((TM, d), lambda j: (j, 0)),
                  pl.BlockSpec((1, 6, d), _mod_index(nt_b, t // TM)),
                  pl.BlockSpec((d, ncol), lambda j: (0, 0)),
                  pl.BlockSpec((1, ncol), lambda j: (0, 0)),
                  tab, tab, tab, vec, vec,
                  pl.BlockSpec((LANES, LANES), lambda j: (0, 0))],
        out_specs=[pl.BlockSpec((TM, ZA_COLS), lambda j: (j, 0)),
                   pl.BlockSpec((TM, NCOL_B), lambda j: (j, 0)),
                   pl.BlockSpec((TM, NCOL_C), lambda j: (j, 0))],
        out_shape=[jax.ShapeDtypeStruct((t, ZA_COLS), BF16),
                   jax.ShapeDtypeStruct((t, NCOL_B), F32),
                   jax.ShapeDtypeStruct((t, NCOL_C), BF16)],
        compiler_params=_cparams(("arbitrary",)),
        name="in_proj",
    )(xs, mods, w_p, b_p, *tabs, q_norm, k_norm, gm)


def _gqa_kernel(q_ref, k_ref, v_ref, o_ref, *, n_keys, ctx_len):
    qi = pl.program_id(2)
    lo_half = lax.broadcasted_iota(jnp.int32, (1, LANES), 1) < HEAD_DIM
    group = ATTN_HEADS // ATTN_KV_HEADS
    zero = jnp.zeros((TM, LANES), BF16)

    def attend(nk):
        outs = []
        for g in range(group):
            xq = q_ref[:, (g // 2) * LANES:(g // 2 + 1) * LANES]
            qm = jnp.where(lo_half if g % 2 == 0 else jnp.logical_not(lo_half), xq, zero)
            s = _nt(qm, k_ref[0:nk, :])
            p = jnp.exp2(s - jnp.max(s, axis=-1, keepdims=True)).astype(BF16)
            outs.append(_dot(p, v_ref[0:nk, (g % 2) * LANES:(g % 2 + 1) * LANES]))
        tiles = []
        for i in range(group // 2):
            num = jnp.where(lo_half, outs[2 * i], outs[2 * i + 1])
            den = pltpu.roll(jnp.where(lo_half, outs[2 * i + 1], outs[2 * i]), HEAD_DIM, 1)
            tiles.append(num / den)
        o_ref[...] = jnp.concatenate(tiles, axis=1).astype(o_ref.dtype)

    pl.when(qi == 0)(lambda: attend(ctx_len))
    pl.when(qi > 0)(lambda: attend(n_keys))


def _gqa_call(za, bsz, tb):
    t = za.shape[0]
    nt_b = tb // TM
    k_blk = A_Q // LANES
    return pl.pallas_call(
        functools.partial(_gqa_kernel, n_keys=tb, ctx_len=TM),
        grid=(bsz, ATTN_KV_HEADS, nt_b),
        in_specs=[pl.BlockSpec((TM, 2 * LANES), lambda b, h, i: (b * nt_b + i, h)),
                  pl.BlockSpec((tb, LANES), lambda b, h, i: (b, k_blk + h)),
                  pl.BlockSpec((tb, 2 * LANES), lambda b, h, i: (b, (k_blk + ATTN_KV_HEADS) // 2 + h))],
        out_specs=pl.BlockSpec((TM, 2 * LANES), lambda b, h, i: (b * nt_b + i, h)),
        out_shape=jax.ShapeDtypeStruct((t, A_Q), BF16),
        compiler_params=_cparams(("arbitrary", "arbitrary", "arbitrary")),
        name="gqa_axial",
    )(za, za, za)


def _gla_kernel(*refs, reverse):
    if reverse:
        (q_ref, k_ref, v_ref, a_ref, r_ref, of_ref, wg_ref, bg_ref, mat_ref, hsel_ref, lvl_ref, vmask_ref,
         ng_ref, o_ref, st_s) = refs
    else:
        q_ref, k_ref, v_ref, a_ref, wg_ref, bg_ref, mat_ref, hsel_ref, lvl_ref, vmask_ref, o_ref, st_s = refs
    C, SUB, H = GLA_CHUNK, GLA_SUB, GLA_HEADS
    rows_t = q_ref.shape[0]

    @pl.when(pl.program_id(1) == 0)
    def _zero_state():
        st_s[...] = jnp.zeros_like(st_s)

    lane_k = lax.broadcasted_iota(jnp.int32, (1, B_QK), 1) // GLA_DK
    hmask = [lane_k == h for h in range(H)]
    lane_s = lax.broadcasted_iota(jnp.int32, (1, LANES), 1) // SUB

    def split3(x):
        x1 = x.astype(BF16)
        r1 = x - x1.astype(F32)
        x2 = r1.astype(BF16)
        return [x1, x2, (r1 - x2.astype(F32)).astype(BF16)]

    def stack_heads(x):
        return jnp.concatenate([jnp.where(hmask[h], x, 0.0) for h in range(H)], axis=0).astype(BF16)

    def head_diag(y, r, w):
        return jnp.concatenate([y[h * r:(h + 1) * r, h * w:(h + 1) * w] for h in range(H)], axis=1)

    q = q_ref[...] * (GLA_DK ** -0.5)
    k = k_ref[...]
    vb = v_ref[...].astype(BF16)
    a = a_ref[...]
    a_hi = a.astype(BF16)
    a_lo = (a - a_hi.astype(F32)).astype(BF16)
    xw = _dot(a_hi, wg_ref[...])
    x = xw[:, 0:B_QK] + xw[:, B_QK:2 * B_QK] + _dot(a_lo, wg_ref[:, 0:B_QK]) + bg_ref[...]
    g = (jnp.minimum(x, 0.0) - jnp.log(1.0 + jnp.exp(-jnp.abs(x)))) * (1.0 / GLA_TAU)
    cm = _dot(mat_ref[...], jnp.concatenate(split3(g), axis=1))
    b = cm[:, 0:B_QK] + cm[:, B_QK:2 * B_QK] + cm[:, 2 * B_QK:3 * B_QK]

    def group_rows(grp, row_in_group):
        return jnp.concatenate([jnp.broadcast_to(b[r0 + row_in_group:r0 + row_in_group + 1, :], (grp, B_QK))
                                for r0 in range(0, rows_t, grp)], axis=0)

    to_end = group_rows(C, 0 if reverse else C - 1) - b
    qb = q * jnp.exp(b)
    kend = (k * jnp.exp(to_end)).astype(BF16)
    q_lv, k_lv = [], []
    for lv in range(GLA_LEVELS):
        grp = C >> lv
        ref = group_rows(grp, grp // 2 if reverse else grp // 2 - 1)
        e_lv = jnp.exp(-jnp.abs(b - ref))
        q_lv.append(q * e_lv)
        k_lv.append((k * e_lv).astype(BF16))
    terms = []
    for d in range(SUB):
        sh = (rows_t - d) % rows_t if reverse else d
        kd = pltpu.roll(k, sh, 0) if sh else k
        bd = pltpu.roll(b, sh, 0) if sh else b
        terms.append((q * kd * jnp.exp(jnp.minimum(b - bd, 0.0))).astype(BF16))
    sacc = jnp.where(vmask_ref[...] != 0.0, _dot(jnp.concatenate(terms, axis=1), hsel_ref[...]), 0.0)
    level = lvl_ref[...]

    st = st_s[...]
    n_chunks = rows_t // C
    order = range(n_chunks - 1, -1, -1) if reverse else range(n_chunks)
    for c in order:
        rows = slice(c * C, (c + 1) * C)
        diag = []
        for h in range(H):
            xh = jnp.where(lane_s == h, sacc[rows, :], 0.0)
            sh = (LANES - h * SUB - (0 if reverse else SUB - 1)) % LANES
            diag.append(pltpu.roll(xh, sh, 1, stride=1, stride_axis=0))
        p = jnp.concatenate(diag, axis=0)[:, 0:C]
        for lv in range(GLA_LEVELS):
            p = jnp.where(level == lv + 1, _nt(stack_heads(q_lv[lv][rows, :]), k_lv[lv][rows, :]), p)
        o = head_diag(_dot(p.astype(BF16), vb[rows, :]), C, GLA_DV)

        o_int = _nt(stack_heads(qb[rows, :]), st.astype(BF16))
        o = o + jnp.concatenate([o_int[h * C:(h + 1) * C, :] for h in range(H)], axis=1)
        upd = lax.dot_general(vb[rows, :], kend[rows, :], (((0,), (0,)), ((), ())),
                              preferred_element_type=F32)
        new = upd[(H - 1) * GLA_DV:H * GLA_DV, :]
        for h in range(H - 2, -1, -1):
            new = jnp.where(hmask[h], upd[h * GLA_DV:(h + 1) * GLA_DV, :], new)
        r_end = c * C if reverse else (c + 1) * C - 1
        st = st * jnp.exp(b[r_end:r_end + 1, :]) + new

        if reverse:
            ot = o + of_ref[rows, :]
            outs = []
            for h in range(H):
                oh = ot[:, h * GLA_DV:(h + 1) * GLA_DV]
                ms = jnp.mean(oh * oh, axis=-1, keepdims=True)
                outs.append(oh * lax.rsqrt(ms + RMS_EPS) * ng_ref[...])
            r = r_ref[rows, :]
            o_ref[rows, :] = (jnp.concatenate(outs, axis=1) * (r * jax.nn.sigmoid(r))).astype(o_ref.dtype)
        else:
            o_ref[rows, :] = o
    st_s[...] = st


def _gla_consts(reverse):
    C, S, H = GLA_CHUNK, GLA_SUB, GLA_HEADS
    assert C == S * 2 ** GLA_LEVELS and H * S <= LANES
    t = np.arange(TM)
    chunk, tt = t // C, t % C
    tau = (C - 1 - tt) if reverse else tt
    same = chunk[:, None] == chunk[None, :]
    incl = (same & (tau[None, :] <= tau[:, None])).astype(np.float32)
    level = np.zeros((C, C), np.int32)
    for lv in range(GLA_LEVELS):
        grp = C >> lv
        later = (tau % grp) >= grp // 2
        tl, ll = tau[:C], later[:C]
        pair = (tl[:, None] // grp == tl[None, :] // grp) & ll[:, None] & ~ll[None, :]
        level[pair] = lv + 1
    hsel = np.zeros((S, B_QK, LANES), np.float32)
    for d in range(S):
        j = d if reverse else S - 1 - d
        for h in range(H):
            hsel[d, h * GLA_DK:(h + 1) * GLA_DK, h * S + j] = 1.0
    lane = np.arange(LANES)
    d_of_lane = (lane % S) if reverse else (S - 1 - lane % S)
    in_blk = ((tt % S)[:, None] + d_of_lane[None, :] <= S - 1) if reverse else ((tt % S)[:, None] >= d_of_lane[None, :])
    vmask = ((lane < H * S)[None, :] & in_blk).astype(np.float32)
    return (jnp.asarray(incl, BF16), jnp.asarray(hsel.reshape(S * B_QK, LANES), BF16),
            jnp.asarray(np.tile(level, (H, 1))), jnp.asarray(vmask))


def _gla_call(zb, o_fwd, wg, bg, ng, bsz, tb, reverse):
    t = zb.shape[0]
    nt_b = tb // TM
    consts = _gla_consts(reverse)
    const_specs = [pl.BlockSpec(c.shape, lambda b, i: (0, 0)) for c in consts]
    if reverse:
        def blk(b, i):
            return b * nt_b + jnp.where(i == 0, 0, nt_b - i)
    else:
        def blk(b, i):
            return b * nt_b + i
    qs = pl.BlockSpec((TM, B_QK), lambda b, i: (blk(b, i), 0))
    ks = pl.BlockSpec((TM, B_QK), lambda b, i: (blk(b, i), 1))
    vs = pl.BlockSpec((TM, B_V), lambda b, i: (blk(b, i), 1))
    rsp = pl.BlockSpec((TM, B_V), lambda b, i: (blk(b, i), 2))
    asp = pl.BlockSpec((TM, LANES), lambda b, i: (blk(b, i), (2 * B_QK + 2 * B_V) // LANES))
    osp = pl.BlockSpec((TM, B_V), lambda b, i: (blk(b, i), 0))
    wsp = pl.BlockSpec((LANES, 2 * B_QK), lambda b, i: (0, 0))
    bsp = pl.BlockSpec((1, B_QK), lambda b, i: (0, 0))
    if reverse:
        in_specs = [qs, ks, vs, asp, rsp, osp, wsp, bsp] + const_specs + [
            pl.BlockSpec((1, GLA_DV), lambda b, i: (0, 0))]
        args = (zb, zb, zb, zb, zb, o_fwd, wg, bg) + consts + (ng,)
        out_dtype = BF16
    else:
        in_specs = [qs, ks, vs, asp, wsp, bsp] + const_specs
        args = (zb, zb, zb, zb, wg, bg) + consts
        out_dtype = F32
    return pl.pallas_call(
        functools.partial(_gla_kernel, reverse=reverse),
        grid=(bsz, nt_b),
        in_specs=in_specs,
        out_specs=osp,
        out_shape=jax.ShapeDtypeStruct((t, B_V), out_dtype),
        scratch_shapes=[pltpu.VMEM((GLA_DV, B_QK), F32)],
        compiler_params=_cparams(("arbitrary", "arbitrary")),
        name="gla_bwd" if reverse else "gla_fwd",
    )(*args)


def _na_kernel(q_ref, k_ref, v_ref, bias_ref, o_ref, *, n_rows, ctx_len):
    i = pl.program_id(1)
    W = GRID_W
    rows_per_step = q_ref.shape[0] // W
    n_ctx_steps = ctx_len // q_ref.shape[0]
    lo_half = lax.broadcasted_iota(jnp.int32, (1, LANES), 1) < HEAD_DIM
    scale = HEAD_DIM ** -0.5
    n_pairs = C_W // LANES

    def pair_queries(q0, nq, pr):
        qp = q_ref[q0:q0 + nq, pr * LANES:(pr + 1) * LANES]
        zero = jnp.zeros_like(qp)
        return jnp.concatenate([jnp.where(lo_half, qp, zero), jnp.where(lo_half, zero, qp)], axis=0)

    def pair_out(o, nq):
        return jnp.where(lo_half, o[0:nq, :], o[nq:2 * nq, :])

    @pl.when(i < n_ctx_steps)
    def _ctx_queries():
        nq = q_ref.shape[0]
        outs = []
        for pr in range(n_pairs):
            cols = slice(pr * LANES, (pr + 1) * LANES)
            s_c = _nt(pair_queries(0, nq, pr), k_ref[0:ctx_len, cols]) * scale
            m = jnp.max(s_c, axis=-1, keepdims=True)
            p = jnp.exp(s_c - m)
            l = jnp.sum(p, axis=-1, keepdims=True)
            outs.append(pair_out(_dot(p.astype(BF16), v_ref[0:ctx_len, cols]) / l, nq))
        o_ref[...] = jnp.concatenate(outs, axis=1).astype(o_ref.dtype)

    @pl.when(i >= n_ctx_steps)
    def _grid_rows():
        for sub in range(rows_per_step):
            r = (i - n_ctx_steps) * rows_per_step + sub
            rs = jnp.clip(r - NA_KH // 2, 0, n_rows - NA_KH)
            start = pl.multiple_of(ctx_len + rs * W, W)
            win = pl.ds(start, NA_KH * W)
            outs = []
            for pr in range(n_pairs):
                cols = slice(pr * LANES, (pr + 1) * LANES)
                qs = pair_queries(sub * W, W, pr)
                s_c = _nt(qs, k_ref[0:ctx_len, cols]) * scale
                s_w = _nt(qs, k_ref[win, cols]) * scale + bias_ref[r - rs, pr * 2 * W:(pr + 1) * 2 * W, :]
                m = jnp.maximum(jnp.max(s_w, axis=-1, keepdims=True), jnp.max(s_c, axis=-1, keepdims=True))
                p_w = jnp.exp(s_w - m)
                p_c = jnp.exp(s_c - m)
                l = jnp.sum(p_w, axis=-1, keepdims=True) + jnp.sum(p_c, axis=-1, keepdims=True)
                o = _dot(p_w.astype(BF16), v_ref[win, cols]) + _dot(p_c.astype(BF16), v_ref[0:ctx_len, cols])
                outs.append(pair_out(o / l, W))
            o_ref[sub * W:(sub + 1) * W, :] = jnp.concatenate(outs, axis=1).astype(o_ref.dtype)


def _na_call(zc, bias_d, bsz, tb):
    t = zc.shape[0]
    q_rows = NA_ROWS_PER_STEP * GRID_W
    steps = tb // q_rows
    n_rows = (tb - TM) // GRID_W
    assert TM % q_rows == 0 and n_rows % NA_ROWS_PER_STEP == 0
    kern = functools.partial(_na_kernel, n_rows=n_rows, ctx_len=TM)
    return pl.pallas_call(
        kern,
        grid=(bsz, steps),
        in_specs=[pl.BlockSpec((q_rows, C_W), lambda b, i: (b * steps + i, 0)),
                  pl.BlockSpec((tb, C_W), lambda b, i: (b, 1)),
                  pl.BlockSpec((tb, C_W), lambda b, i: (b, 2)),
                  pl.BlockSpec(bias_d.shape, lambda b, i: (0, 0, 0))],
        out_specs=pl.BlockSpec((q_rows, C_W), lambda b, i: (b * steps + i, 0)),
        out_shape=jax.ShapeDtypeStruct((t, C_W), BF16),
        compiler_params=_cparams(("arbitrary", "arbitrary")),
        name="nbr_attn",
    )(zc, zc, zc, bias_d)


def _merge_kernel(x_ref, mod_ref, oa_ref, ob_ref, oc_ref, wg_ref, bg_ref, wa_ref, wb_ref, wc_ref,
                  wo_ref, lg_ref, lb_ref, wr_ref, br_ref, x1_ref, h2_ref, rt_ref, *, alpha):
    d = x_ref.shape[1]
    x = x_ref[...]
    h = _modulate(x, mod_ref[0, 0:1, :], mod_ref[0, 1:2, :]).astype(BF16)
    y = jnp.zeros_like(x)
    for bi, (o_ref, w_ref) in enumerate(((oa_ref, wa_ref), (ob_ref, wb_ref), (oc_ref, wc_ref))):
        gate = jax.nn.sigmoid(_dot(h, wg_ref[:, bi * d:(bi + 1) * d]) + bg_ref[:, bi * d:(bi + 1) * d])
        y = y + gate * _dot(o_ref[...], w_ref[...])
    y2 = _dot(y.astype(BF16), wo_ref[...])
    x1 = _ln0(alpha * x + mod_ref[0, 2:3, :] * y2) * lg_ref[...] + lb_ref[...]
    x1_ref[...] = x1
    h2 = _modulate(x1, mod_ref[0, 3:4, :], mod_ref[0, 4:5, :])
    h2_ref[...] = h2

    h2_hi = h2.astype(BF16)
    h2_lo = (h2 - h2_hi.astype(F32)).astype(BF16)
    hw = _dot(h2_hi, wr_ref[...])
    logits = hw[:, 0:LANES] + hw[:, LANES:2 * LANES] + _dot(h2_lo, wr_ref[:, 0:LANES]) + br_ref[...]
    lane = lax.broadcasted_iota(jnp.int32, logits.shape, 1)
    big = jnp.int32(1 << 20)
    neg = jnp.float32(-jnp.inf)

    def amax(vals):
        m = jnp.max(vals, axis=-1, keepdims=True)
        return m, jnp.min(jnp.where(vals == m, lane, big), axis=-1, keepdims=True)

    gl = jnp.where(lane < N_GROUPS, logits, neg)
    gmax, grp = amax(gl)
    p_grp = 1.0 / jnp.sum(jnp.exp(gl - gmax), axis=-1, keepdims=True)
    lo = N_GROUPS + grp * EXPERTS_PER_GROUP
    sl = jnp.where((lane >= lo) & (lane < lo + EXPERTS_PER_GROUP), logits, neg)
    v1, i1 = amax(sl)
    v2, i2 = amax(jnp.where(lane == i1, neg, sl))
    e2 = jnp.exp(v2 - v1)
    g1 = p_grp / (1.0 + e2)
    g2 = p_grp * e2 / (1.0 + e2)
    rt = jnp.where(lane == 0, (i1 - N_GROUPS).astype(F32),
                   jnp.where(lane == 1, (i2 - N_GROUPS).astype(F32),
                             jnp.where(lane == 2, g1, jnp.where(lane == 3, g2, 0.0))))
    rt_ref[...] = rt


def _merge_call(xs, mods, oa, ob, oc, wgate, bgate, wa, wb, wc, wo, lg, lb, wr, br, nt_b, alpha):
    t, d = xs.shape
    full = lambda shape: pl.BlockSpec(shape, lambda j: tuple(0 for _ in shape))
    row = lambda w: pl.BlockSpec((TM, w), lambda j: (j, 0))
    return pl.pallas_call(
        functools.partial(_merge_kernel, alpha=alpha),
        grid=(t // TM,),
        in_specs=[row(d), pl.BlockSpec((1, 6, d), _mod_index(nt_b, t // TM)), row(A_Q), row(B_V), row(C_W),
                  full(wgate.shape), full(bgate.shape), full(wa.shape), full(wb.shape), full(wc.shape),
                  full(wo.shape), full(lg.shape), full(lb.shape), full(wr.shape), full(br.shape)],
        out_specs=[row(d), row(d), row(LANES)],
        out_shape=[jax.ShapeDtypeStruct((t, d), F32), jax.ShapeDtypeStruct((t, d), F32),
                   jax.ShapeDtypeStruct((t, LANES), F32)],
        compiler_params=_cparams(("arbitrary",)),
        name="merge_router",
    )(xs, mods, oa, ob, oc, wgate, bgate, wa, wb, wc, wo, lg, lb, wr, br)


def _dispatch_kernel(pos_ref, h_ref, xs_in_ref, xs_ref, sem):
    del xs_in_ref

    def row_copy(tok, slot):
        return pltpu.make_async_copy(h_ref.at[pl.ds(tok, 1)], xs_ref.at[pl.ds(slot, 1)], sem)

    def issue(tok, carry):
        for kk in range(TOP_K):
            row_copy(tok, pos_ref[0, 0, TOP_K * tok + kk]).start(priority=kk)
        return carry

    lax.fori_loop(0, TM, issue, 0, unroll=DMA_UNROLL)
    for kk in range(TOP_K):
        pltpu.make_async_copy(h_ref, xs_ref.at[pl.ds(0, TM)], sem).wait()


def _dispatch_call(pos3, h2, n_slots):
    t, d = h2.shape
    zeros = jnp.zeros((n_slots, d), F32)
    return pl.pallas_call(
        _dispatch_kernel,
        grid=(t // TM,),
        in_specs=[pl.BlockSpec((1, 1, TOP_K * TM), lambda j: (j, 0, 0), memory_space=pltpu.SMEM),
                  pl.BlockSpec((TM, d), lambda j: (j, 0)),
                  pl.BlockSpec(memory_space=pl.ANY)],
        out_specs=pl.BlockSpec(memory_space=pl.ANY),
        out_shape=jax.ShapeDtypeStruct((n_slots, d), F32),
        scratch_shapes=[pltpu.SemaphoreType.DMA(())],
        input_output_aliases={2: 0},
        compiler_params=_cparams(("arbitrary",)),
        name="moe_dispatch",
    )(pos3, h2, zeros)


def _expert_kernel(be_ref, x_ref, w1_ref, w3_ref, w2_ref, y_ref):
    del be_ref
    xb = x_ref[...].astype(BF16)
    a = _dot(xb, w1_ref[0].astype(BF16))
    b = _dot(xb, w3_ref[0].astype(BF16))
    mid = (a * jax.nn.sigmoid(a) * b).astype(BF16)
    y_ref[...] = _dot(mid, w2_ref[0].astype(BF16))


def _expert_call(block_expert, x_slots, w1, w3, w2):
    n_slots, d = x_slots.shape
    de = w1.shape[2]
    grid_spec = pltpu.PrefetchScalarGridSpec(
        num_scalar_prefetch=1,
        grid=(n_slots // MOE_BLK,),
        in_specs=[pl.BlockSpec((MOE_BLK, d), lambda j, be: (j, 0)),
                  pl.BlockSpec((1, d, de), lambda j, be: (be[j], 0, 0)),
                  pl.BlockSpec((1, d, de), lambda j, be: (be[j], 0, 0)),
                  pl.BlockSpec((1, de, d), lambda j, be: (be[j], 0, 0))],
        out_specs=pl.BlockSpec((MOE_BLK, d), lambda j, be: (j, 0)),
    )
    return pl.pallas_call(
        _expert_kernel,
        grid_spec=grid_spec,
        out_shape=jax.ShapeDtypeStruct((n_slots, d), F32),
        compiler_params=_cparams(("arbitrary",)),
        name="moe_experts",
    )(block_expert, x_slots, w1, w3, w2)


def _combine_kernel(pos_ref, x1_ref, mod_ref, rt_ref, ys_ref, lg_ref, lb_ref, o_ref, buf, sem, *, alpha):
    def row_copy(slot, kk, tok):
        return pltpu.make_async_copy(ys_ref.at[pl.ds(slot, 1)], buf.at[kk, pl.ds(tok, 1)], sem)

    def issue(tok, carry):
        for kk in range(TOP_K):
            row_copy(pos_ref[0, 0, TOP_K * tok + kk], kk, tok).start(priority=kk)
        return carry

    lax.fori_loop(0, TM, issue, 0, unroll=DMA_UNROLL)
    for kk in range(TOP_K):
        pltpu.make_async_copy(ys_ref.at[pl.ds(0, TM)], buf.at[kk], sem).wait()
    rt = rt_ref[...]
    y = rt[:, 2:3] * buf[0] + rt[:, 3:4] * buf[1]
    x2 = _ln0(alpha * x1_ref[...] + mod_ref[0, 5:6, :] * y) * lg_ref[...] + lb_ref[...]
    o_ref[...] = x2


def _combine_call(pos3, x1, mods, rt, y_slots, lg, lb, nt_b, alpha, latent_only):
    t, d = x1.shape
    vec = pl.BlockSpec((1, d), lambda j: (0, 0))
    if latent_only:
        n_lat = nt_b - 1
        n_out = (t // TM // nt_b) * n_lat
        tile = lambda j: (j // n_lat) * nt_b + 1 + j % n_lat
        mod_idx = lambda j: (j // n_lat, 0, 0)
    else:
        n_out = t // TM
        tile = lambda j: j
        mod_idx = _mod_index(nt_b, t // TM)
    return pl.pallas_call(
        functools.partial(_combine_kernel, alpha=alpha),
        grid=(n_out,),
        in_specs=[pl.BlockSpec((1, 1, TOP_K * TM), lambda j: (tile(j), 0, 0), memory_space=pltpu.SMEM),
                  pl.BlockSpec((TM, d), lambda j: (tile(j), 0)),
                  pl.BlockSpec((1, 6, d), mod_idx),
                  pl.BlockSpec((TM, LANES), lambda j: (tile(j), 0)),
                  pl.BlockSpec(memory_space=pl.ANY), vec, vec],
        out_specs=pl.BlockSpec((TM, d), lambda j: (j, 0)),
        out_shape=jax.ShapeDtypeStruct((n_out * TM, d), F32),
        scratch_shapes=[pltpu.VMEM((TOP_K, TM, d), F32), pltpu.SemaphoreType.DMA(())],
        compiler_params=_cparams(("arbitrary",)),
        name="moe_combine",
    )(pos3, x1, mods, rt, y_slots, lg, lb)


def _slot_plan(rt, n_blocks):
    flat_e = rt[:, 0:TOP_K].astype(jnp.int32).reshape(-1)
    onehot = (flat_e[:, None] == jnp.arange(N_EXPERTS, dtype=jnp.int32)[None, :]).astype(jnp.int32)
    csum = jnp.cumsum(onehot, axis=0)
    counts = csum[-1]
    padded = ((counts + MOE_BLK - 1) // MOE_BLK) * MOE_BLK
    pad_end = jnp.cumsum(padded)
    pad_start = pad_end - padded
    pos = jnp.sum(onehot * (csum - 1 + pad_start[None, :]), axis=1)
    block_start = jnp.arange(n_blocks, dtype=jnp.int32) * MOE_BLK
    block_expert = jnp.minimum(
        jnp.sum((pad_end[None, :] <= block_start[:, None]).astype(jnp.int32), axis=1), N_EXPERTS - 1)
    return pos.astype(jnp.int32), block_expert.astype(jnp.int32)


def _rope_tables(seq_len, ctx_len):
    n_freq = HEAD_DIM // 4
    t = jnp.arange(seq_len)
    invf = ROPE_THETA ** (-jnp.arange(n_freq, dtype=F32) / n_freq)
    ar = (t // GRID_W).astype(F32)[:, None] * invf
    ac = (t % GRID_W).astype(F32)[:, None] * invf
    ang = jnp.concatenate([ar, ar, ac, ac], axis=-1)
    cos, sin = jnp.cos(ang), jnp.sin(ang)
    quarter = (np.arange(HEAD_DIM) // n_freq) % 2
    sa = sin * jnp.asarray(quarter == 1, F32)
    sb = -sin * jnp.asarray(quarter == 0, F32)
    ones = jnp.ones((ctx_len, HEAD_DIM), F32)
    zeros = jnp.zeros((ctx_len, HEAD_DIM), F32)
    tabs = []
    for lat, ctx in ((cos, ones), (sa, zeros), (sb, zeros)):
        tab = jnp.concatenate([ctx, lat], axis=0)
        tabs.append(jnp.tile(tab, (1, LANES // HEAD_DIM)))
    return tabs


def _na_bias_table(rpb):
    kh = NA_KH
    w = np.arange(GRID_W)
    col_start = np.clip(w - NA_KW // 2, 0, GRID_W - NA_KW)
    colk = np.arange(GRID_W)
    inside = (colk[None, :] >= col_start[:, None]) & (colk[None, :] < col_start[:, None] + NA_KW)
    col_rel = colk[None, :] - w[:, None] + (NA_KW - 1)
    pats = np.arange(kh)
    row_rel = np.arange(kh)[None, :] - pats[:, None] + (NA_KH - 1)
    row_sel = (row_rel[:, :, None] == np.arange(2 * NA_KH - 1)).astype(np.float32)
    col_sel = ((col_rel[:, :, None] == np.arange(2 * NA_KW - 1)) & inside[:, :, None]).astype(np.float32)
    b = jnp.einsum('lhab,pia,wcb->lphwic', rpb, jnp.asarray(row_sel), jnp.asarray(col_sel), precision=HIGHEST)
    b = jnp.where(jnp.asarray(inside)[None, None, None, :, None, :], b, NEG_BIG)
    return b.reshape(rpb.shape[0], kh, NA_HEADS * GRID_W, kh * GRID_W).astype(F32)


def _pack_in_proj(w_in, b_in):
    a0 = 0
    a1 = a0 + NCOL_A
    bq0 = a1
    b_end = bq0 + 2 * B_QK + 2 * B_V + B_A
    c_end = b_end + NCOL_C
    d = w_in.shape[0]
    padw = jnp.zeros((d, LANES - B_A), w_in.dtype)
    padb = jnp.zeros((LANES - B_A,), b_in.dtype)
    w_p = jnp.concatenate([w_in[:, a0:a1], w_in[:, bq0:b_end], padw, w_in[:, b_end:c_end]], axis=1)
    b_p = jnp.concatenate([b_in[a0:a1], b_in[bq0:b_end], padb, b_in[b_end:c_end]])
    return w_p.astype(BF16), b_p.reshape(1, -1), w_in[:, c_end:].astype(BF16), b_in[c_end:].reshape(1, -1)


def kernel(x, c, ctx, c_ctx, w_mod, b_mod, w_in, b_in, attn_q_norm, attn_k_norm, gla_w_gate, gla_b_gate,
           gla_norm, na_rpb, w_br_attn, w_br_gla, w_br_na, w_out, ln1_g, ln1_b, w_router_group,
           b_router_group, w_router_expert, b_router_expert, moe_w1, moe_w3, moe_w2, ln2_g, ln2_b):
    bsz, seq, d = x.shape
    ctx_len = ctx.shape[1]
    depth = w_mod.shape[0]
    assert ctx_len == TM and seq % TM == 0 and (seq // GRID_W) >= NA_KH
    tb = ctx_len + seq
    nt_b = tb // TM
    t = bsz * tb
    alpha = (2 * depth) ** 0.25

    xs = jnp.concatenate([ctx, x], axis=1).reshape(t, d)

    cin = jnp.zeros((8, d), F32).at[0:bsz].set(c).at[bsz].set(c_ctx)
    mod_all = _mod_call(cin, w_mod, b_mod)

    tabs = _rope_tables(seq, ctx_len)
    gm = jnp.asarray(np.kron(np.eye(LANES // HEAD_DIM), np.ones((HEAD_DIM, HEAD_DIM))) / HEAD_DIM, BF16)
    n_assign = t * TOP_K
    n_blocks = -(-n_assign // MOE_BLK) + N_EXPERTS
    n_slots = n_blocks * MOE_BLK
    na_bias = _na_bias_table(na_rpb)
    w1_all = moe_w1.reshape((depth * N_EXPERTS,) + moe_w1.shape[2:])
    w3_all = moe_w3.reshape((depth * N_EXPERTS,) + moe_w3.shape[2:])
    w2_all = moe_w2.reshape((depth * N_EXPERTS,) + moe_w2.shape[2:])

    for l in range(depth):
        m = mod_all[l].reshape(8, 6, d)
        mods = m

        w_p, b_p, w_gate_cols, b_gate_cols = _pack_in_proj(w_in[l], b_in[l])
        qn = jnp.tile(attn_q_norm[l], LANES // HEAD_DIM).reshape(1, LANES)
        kn = jnp.tile(attn_k_norm[l], LANES // HEAD_DIM).reshape(1, LANES)
        za, zb, zc = _proj_call(xs, mods, w_p, b_p, tabs, qn, kn, gm, nt_b)
        oa = _gqa_call(za, bsz, tb)

        o_f = None
        for direction in range(2):
            wg = jnp.zeros((LANES, B_QK), F32).at[direction * GLA_GATE_RANK:(direction + 1) * GLA_GATE_RANK].set(
                gla_w_gate[l, direction])
            wg_hi = wg.astype(BF16)
            wg = jnp.concatenate([wg_hi, (wg - wg_hi.astype(F32)).astype(BF16)], axis=1)
            bg = gla_b_gate[l, direction].reshape(1, B_QK)
            o_f = _gla_call(zb, o_f, wg, bg, gla_norm[l].reshape(1, GLA_DV), bsz, tb, reverse=direction == 1)
        ob = o_f

        oc = _na_call(zc, na_bias[l], bsz, tb)

        wr = jnp.concatenate([w_router_group[l], w_router_expert[l],
                              jnp.zeros((d, LANES - N_GROUPS - N_EXPERTS), F32)], axis=1)
        br = jnp.concatenate([b_router_group[l], b_router_expert[l],
                              jnp.zeros((LANES - N_GROUPS - N_EXPERTS,), F32)]).reshape(1, LANES)
        wr_hi = wr.astype(BF16)
        wr = jnp.concatenate([wr_hi, (wr - wr_hi.astype(F32)).astype(BF16)], axis=1)
        x1, h2, rt = _merge_call(xs, mods, oa, ob, oc, w_gate_cols, b_gate_cols,
                                 w_br_attn[l].astype(BF16), w_br_gla[l].astype(BF16), w_br_na[l].astype(BF16),
                                 w_out[l].astype(BF16), ln1_g[l].reshape(1, d), ln1_b[l].reshape(1, d),
                                 wr, br, nt_b, alpha)

        pos, block_expert = _slot_plan(rt, n_blocks)
        pos3 = pos.reshape(t // TM, 1, TOP_K * TM)
        x_slots = _dispatch_call(pos3, h2, n_slots)
        y_slots = _expert_call(block_expert + l * N_EXPERTS, x_slots, w1_all, w3_all, w2_all)
        xs = _combine_call(pos3, x1, mods, rt, y_slots, ln2_g[l].reshape(1, d), ln2_b[l].reshape(1, d),
                           nt_b, alpha, latent_only=l == depth - 1)

    return xs.reshape(bsz, seq, d)
```

```python
import functools

import jax
import jax.numpy as jnp
import numpy as np
from jax import lax
from jax.experimental import pallas as pl
from jax.experimental.pallas import tpu as pltpu

F32 = jnp.float32
BF16 = jnp.bfloat16
HIGHEST = lax.Precision.HIGHEST

GRID_W = 64
HEAD_DIM = 64
ROPE_THETA = 10000.0
ATTN_HEADS = 8
ATTN_KV_HEADS = 2
GLA_HEADS = 4
GLA_DK = 64
GLA_DV = 128
GLA_GATE_RANK = 16
GLA_TAU = 16.0
GLA_CHUNK = 64
GLA_SUB = 8
GLA_LEVELS = 3
NA_HEADS = 8
NA_KH = 8
NA_KW = 16
NA_ROWS_PER_STEP = 4
N_GROUPS = 4
EXPERTS_PER_GROUP = 8
N_EXPERTS = N_GROUPS * EXPERTS_PER_GROUP
TOP_K = 2
LN_EPS = 1e-6
RMS_EPS = 1e-6
NEG_BIG = -1e30
LOG2E = 1.4426950408889634

LANES = 128
TM = 256
MOE_BLK = 256
DMA_UNROLL = 8
VMEM_LIMIT = 56 * 1024 * 1024

A_Q, A_KV = ATTN_HEADS * HEAD_DIM, ATTN_KV_HEADS * HEAD_DIM
B_QK, B_V, B_A = GLA_HEADS * GLA_DK, GLA_HEADS * GLA_DV, 2 * GLA_GATE_RANK
C_W = NA_HEADS * HEAD_DIM
NCOL_A = A_Q + 2 * A_KV
ZA_COLS = A_Q + 3 * ATTN_KV_HEADS * LANES
NCOL_B = 2 * B_QK + 2 * B_V + LANES
NCOL_C = 3 * C_W


def _cparams(sem):
    return pltpu.CompilerParams(dimension_semantics=sem, vmem_limit_bytes=VMEM_LIMIT)


def _ln0(x):
    mu = jnp.mean(x, axis=-1, keepdims=True)
    xc = x - mu
    var = jnp.mean(xc * xc, axis=-1, keepdims=True)
    return xc * lax.rsqrt(var + LN_EPS)


def _modulate(x, shift, scale):
    return _ln0(x) * (1.0 + scale) + shift


def _nt(a, b):
    return lax.dot_general(a, b, (((1,), (1,)), ((), ())), preferred_element_type=F32)


def _dot(a, b, precision=None):
    return jnp.dot(a, b, preferred_element_type=F32, precision=precision)


def _mod_kernel(c_ref, w_ref, b_ref, o_ref):
    c = c_ref[...]
    s = c * jax.nn.sigmoid(c)
    o_ref[0] = _dot(s, w_ref[0]) + b_ref[0]


def _mod_call(cin, w_mod, b_mod):
    depth, d, nmod = w_mod.shape
    nb = nmod // d
    return pl.pallas_call(
        _mod_kernel,
        grid=(depth, nb),
        in_specs=[pl.BlockSpec((8, d), lambda l, n: (0, 0)),
                  pl.BlockSpec((1, d, d), lambda l, n: (l, 0, n)),
                  pl.BlockSpec((1, 1, d), lambda l, n: (l, 0, n))],
        out_specs=pl.BlockSpec((1, 8, d), lambda l, n: (l, 0, n)),
        out_shape=jax.ShapeDtypeStruct((depth, 8, nmod), F32),
        compiler_params=_cparams(("arbitrary", "arbitrary")),
        name="adaln_mod",
    )(cin, w_mod, b_mod.reshape(depth, 1, nmod))


def _proj_kernel(x_ref, mod_ref, w_ref, b_ref, cos_ref, sa_ref, sb_ref, qn_ref, kn_ref, gm_ref,
                 za_ref, zb_ref, zc_ref):
    h = _modulate(x_ref[...], mod_ref[0, 0:1, :], mod_ref[0, 1:2, :]).astype(BF16)
    lo = NCOL_A
    for o_ref in (zb_ref, zc_ref):
        hi = lo + o_ref.shape[1]
        acc = _dot(h, w_ref[:, lo:hi]) + b_ref[:, lo:hi]
        o_ref[...] = acc.astype(o_ref.dtype)
        lo = hi

    gm = gm_ref[...]
    lo_half = lax.broadcasted_iota(jnp.int32, (1, LANES), 1) < HEAD_DIM

    def rms(x, g):
        x2 = x * x
        hi2 = x2.astype(BF16)
        lo2 = (x2 - hi2.astype(F32)).astype(BF16)
        ms = _dot(hi2, gm) + _dot(lo2, gm)
        return x * lax.rsqrt(ms + RMS_EPS) * g

    def rope(x):
        q4 = HEAD_DIM // 4
        return x * cos_ref[...] + pltpu.roll(x, q4, 1) * sa_ref[...] + pltpu.roll(x, LANES - q4, 1) * sb_ref[...]

    def both_halves(x):
        sw = pltpu.roll(x, HEAD_DIM, 1)
        return [jnp.where(lo_half, x, sw), jnp.where(lo_half, sw, x)]

    acc = _dot(h, w_ref[:, 0:NCOL_A]) + b_ref[:, 0:NCOL_A]
    parts = []
    for i in range(A_Q // LANES):
        parts.append(rope(rms(acc[:, i * LANES:(i + 1) * LANES], qn_ref[...])) * (HEAD_DIM ** -0.5 * LOG2E))
    parts += both_halves(rope(rms(acc[:, A_Q:A_Q + A_KV], kn_ref[...])))
    for vv in both_halves(acc[:, A_Q + A_KV:A_Q + 2 * A_KV]):
        parts += [jnp.where(lo_half, vv, 1.0), jnp.where(lo_half, 1.0, vv)]
    za_ref[...] = jnp.concatenate(parts, axis=1).astype(za_ref.dtype)


def _mod_index(nt_b, n_tiles):
    ctx_row = n_tiles // nt_b
    return lambda j: (jnp.where(j % nt_b == 0, ctx_row, j // nt_b), 0, 0)


def _proj_call(xs, mods, w_p, b_p, tabs, q_norm, k_norm, gm, nt_b):
    t, d = xs.shape
    ncol = w_p.shape[1]
    tab = pl.BlockSpec((TM, LANES), lambda j: (j % nt_b, 0))
    vec = pl.BlockSpec((1, LANES), lambda j: (0, 0))
    return pl.pallas_call(
        _proj_kernel,
        grid=(t // TM,),
        in_specs=[pl.BlockSpec((TM, d), lambda j: (j, 0)),
                  pl.BlockSpec((1, 6, d), _mod_index(nt_b, t // TM)),
                  pl.BlockSpec((d, ncol), lambda j: (0, 0)),
                  pl.BlockSpec((1, ncol), lambda j: (0, 0)),
                  tab, tab, tab, vec, vec,
                  pl.BlockSpec((LANES, LANES), lambda j: (0, 0))],
        out_specs=[pl.BlockSpec((TM, ZA_COLS), lambda j: (j, 0)),
                   pl.BlockSpec((TM, NCOL_B), lambda j: (j, 0)),
                   pl.BlockSpec((TM, NCOL_C), lambda j: (j, 0))],
        out_shape=[jax.ShapeDtypeStruct((t, ZA_COLS), BF16),
                   jax.ShapeDtypeStruct((t, NCOL_B), F32),
                   jax.ShapeDtypeStruct((t, NCOL_C), BF16)],
        compiler_params=_cparams(("arbitrary",)),
        name="in_proj",
    )(xs, mods, w_p, b_p, *tabs, q_norm, k_norm, gm)


def _gqa_kernel(q_ref, k_ref, v_ref, o_ref, *, n_keys, ctx_len):
    qi = pl.program_id(2)
    lo_half = lax.broadcasted_iota(jnp.int32, (1, LANES), 1) < HEAD_DIM
    group = ATTN_HEADS // ATTN_KV_HEADS
    zero = jnp.zeros((TM, LANES), BF16)

    def attend(nk):
        outs = []
        for g in range(group):
            xq = q_ref[:, (g // 2) * LANES:(g // 2 + 1) * LANES]
            qm = jnp.where(lo_half if g % 2 == 0 else jnp.logical_not(lo_half), xq, zero)
            s = _nt(qm, k_ref[0:nk, :])
            p = jnp.exp2(s - jnp.max(s, axis=-1, keepdims=True)).astype(BF16)
            outs.append(_dot(p, v_ref[0:nk, (g % 2) * LANES:(g % 2 + 1) * LANES]))
        tiles = []
        for i in range(group // 2):
            num = jnp.where(lo_half, outs[2 * i], outs[2 * i + 1])
            den = pltpu.roll(jnp.where(lo_half, outs[2 * i + 1], outs[2 * i]), HEAD_DIM, 1)
            tiles.append(num / den)
        o_ref[...] = jnp.concatenate(tiles, axis=1).astype(o_ref.dtype)

    pl.when(qi == 0)(lambda: attend(ctx_len))
    pl.when(qi > 0)(lambda: attend(n_keys))


def _gqa_call(za, bsz, tb):
    t = za.shape[0]
    nt_b = tb // TM
    k_blk = A_Q // LANES
    return pl.pallas_call(
        functools.partial(_gqa_kernel, n_keys=tb, ctx_len=TM),
        grid=(bsz, ATTN_KV_HEADS, nt_b),
        in_specs=[pl.BlockSpec((TM, 2 * LANES), lambda b, h, i: (b * nt_b + i, h)),
                  pl.BlockSpec((tb, LANES), lambda b, h, i: (b, k_blk + h)),
                  pl.BlockSpec((tb, 2 * LANES), lambda b, h, i: (b, (k_blk + ATTN_KV_HEADS) // 2 + h))],
        out_specs=pl.BlockSpec((TM, 2 * LANES), lambda b, h, i: (b * nt_b + i, h)),
        out_shape=jax.ShapeDtypeStruct((t, A_Q), BF16),
        compiler_params=_cparams(("arbitrary", "arbitrary", "arbitrary")),
        name="gqa_axial",
    )(za, za, za)


def _gla_kernel(*refs, reverse):
    if reverse:
        (q_ref, k_ref, v_ref, a_ref, r_ref, of_ref, wg_ref, bg_ref, mat_ref, hsel_ref, lvl_ref, vmask_ref,
         ng_ref, o_ref, st_s) = refs
    else:
        q_ref, k_ref, v_ref, a_ref, wg_ref, bg_ref, mat_ref, hsel_ref, lvl_ref, vmask_ref, o_ref, st_s = refs
    C, SUB, H = GLA_CHUNK, GLA_SUB, GLA_HEADS
    rows_t = q_ref.shape[0]

    @pl.when(pl.program_id(1) == 0)
    def _zero_state():
        st_s[...] = jnp.zeros_like(st_s)

    lane_k = lax.broadcasted_iota(jnp.int32, (1, B_QK), 1) // GLA_DK
    hmask = [lane_k == h for h in range(H)]
    lane_s = lax.broadcasted_iota(jnp.int32, (1, LANES), 1) // SUB

    def split3(x):
        x1 = x.astype(BF16)
        r1 = x - x1.astype(F32)
        x2 = r1.astype(BF16)
        return [x1, x2, (r1 - x2.astype(F32)).astype(BF16)]

    def stack_heads(x):
        return jnp.concatenate([jnp.where(hmask[h], x, 0.0) for h in range(H)], axis=0).astype(BF16)

    def head_diag(y, r, w):
        return jnp.concatenate([y[h * r:(h + 1) * r, h * w:(h + 1) * w] for h in range(H)], axis=1)

    q = q_ref[...] * (GLA_DK ** -0.5)
    k = k_ref[...]
    vb = v_ref[...].astype(BF16)
    a = a_ref[...]
    a_hi = a.astype(BF16)
    a_lo = (a - a_hi.astype(F32)).astype(BF16)
    xw = _dot(a_hi, wg_ref[...])
    x = xw[:, 0:B_QK] + xw[:, B_QK:2 * B_QK] + _dot(a_lo, wg_ref[:, 0:B_QK]) + bg_ref[...]
    g = (jnp.minimum(x, 0.0) - jnp.log(1.0 + jnp.exp(-jnp.abs(x)))) * (1.0 / GLA_TAU)
    cm = _dot(mat_ref[...], jnp.concatenate(split3(g), axis=1))
    b = cm[:, 0:B_QK] + cm[:, B_QK:2 * B_QK] + cm[:, 2 * B_QK:3 * B_QK]

    def group_rows(grp, row_in_group):
        return jnp.concatenate([jnp.broadcast_to(b[r0 + row_in_group:r0 + row_in_group + 1, :], (grp, B_QK))
                                for r0 in range(0, rows_t, grp)], axis=0)

    to_end = group_rows(C, 0 if reverse else C - 1) - b
    qb = q * jnp.exp(b)
    kend = (k * jnp.exp(to_end)).astype(BF16)
    q_lv, k_lv = [], []
    for lv in range(GLA_LEVELS):
        grp = C >> lv
        ref = group_rows(grp, grp // 2 if reverse else grp // 2 - 1)
        e_lv = jnp.exp(-jnp.abs(b - ref))
        q_lv.append(q * e_lv)
        k_lv.append((k * e_lv).astype(BF16))
    terms = []
    for d in range(SUB):
        sh = (rows_t - d) % rows_t if reverse else d
        kd = pltpu.roll(k, sh, 0) if sh else k
        bd = pltpu.roll(b, sh, 0) if sh else b
        terms.append((q * kd * jnp.exp(jnp.minimum(b - bd, 0.0))).astype(BF16))
    sacc = jnp.where(vmask_ref[...] != 0.0, _dot(jnp.concatenate(terms, axis=1), hsel_ref[...]), 0.0)
    level = lvl_ref[...]

    st = st_s[...]
    n_chunks = rows_t // C
    order = range(n_chunks - 1, -1, -1) if reverse else range(n_chunks)
    for c in order:
        rows = slice(c * C, (c + 1) * C)
        diag = []
        for h in range(H):
            xh = jnp.where(lane_s == h, sacc[rows, :], 0.0)
            sh = (LANES - h * SUB - (0 if reverse else SUB - 1)) % LANES
            diag.append(pltpu.roll(xh, sh, 1, stride=1, stride_axis=0))
        p = jnp.concatenate(diag, axis=0)[:, 0:C]
        for lv in range(GLA_LEVELS):
            p = jnp.where(level == lv + 1, _nt(stack_heads(q_lv[lv][rows, :]), k_lv[lv][rows, :]), p)
        o = head_diag(_dot(p.astype(BF16), vb[rows, :]), C, GLA_DV)

        o_int = _nt(stack_heads(qb[rows, :]), st.astype(BF16))
        o = o + jnp.concatenate([o_int[h * C:(h + 1) * C, :] for h in range(H)], axis=1)
        upd = lax.dot_general(vb[rows, :], kend[rows, :], (((0,), (0,)), ((), ())),
                              preferred_element_type=F32)
        new = upd[(H - 1) * GLA_DV:H * GLA_DV, :]
        for h in range(H - 2, -1, -1):
            new = jnp.where(hmask[h], upd[h * GLA_DV:(h + 1) * GLA_DV, :], new)
        r_end = c * C if reverse else (c + 1) * C - 1
        st = st * jnp.exp(b[r_end:r_end + 1, :]) + new

        if reverse:
            ot = o + of_ref[rows, :]
            outs = []
            for h in range(H):
                oh = ot[:, h * GLA_DV:(h + 1) * GLA_DV]
                ms = jnp.mean(oh * oh, axis=-1, keepdims=True)
                outs.append(oh * lax.rsqrt(ms + RMS_EPS) * ng_ref[...])
            r = r_ref[rows, :]
            o_ref[rows, :] = (jnp.concatenate(outs, axis=1) * (r * jax.nn.sigmoid(r))).astype(o_ref.dtype)
        else:
            o_ref[rows, :] = o
    st_s[...] = st


def _gla_consts(reverse):
    C, S, H = GLA_CHUNK, GLA_SUB, GLA_HEADS
    assert C == S * 2 ** GLA_LEVELS and H * S <= LANES
    t = np.arange(TM)
    chunk, tt = t // C, t % C
    tau = (C - 1 - tt) if reverse else tt
    same = chunk[:, None] == chunk[None, :]
    incl = (same & (tau[None, :] <= tau[:, None])).astype(np.float32)
    level = np.zeros((C, C), np.int32)
    for lv in range(GLA_LEVELS):
        grp = C >> lv
        later = (tau % grp) >= grp // 2
        tl, ll = tau[:C], later[:C]
        pair = (tl[:, None] // grp == tl[None, :] // grp) & ll[:, None] & ~ll[None, :]
        level[pair] = lv + 1
    hsel = np.zeros((S, B_QK, LANES), np.float32)
    for d in range(S):
        j = d if reverse else S - 1 - d
        for h in range(H):
            hsel[d, h * GLA_DK:(h + 1) * GLA_DK, h * S + j] = 1.0
    lane = np.arange(LANES)
    d_of_lane = (lane % S) if reverse else (S - 1 - lane % S)
    in_blk = ((tt % S)[:, None] + d_of_lane[None, :] <= S - 1) if reverse else ((tt % S)[:, None] >= d_of_lane[None, :])
    vmask = ((lane < H * S)[None, :] & in_blk).astype(np.float32)
    return (jnp.asarray(incl, BF16), jnp.asarray(hsel.reshape(S * B_QK, LANES), BF16),
            jnp.asarray(np.tile(level, (H, 1))), jnp.asarray(vmask))


def _gla_call(zb, o_fwd, wg, bg, ng, bsz, tb, reverse):
    t = zb.shape[0]
    nt_b = tb // TM
    consts = _gla_consts(reverse)
    const_specs = [pl.BlockSpec(c.shape, lambda b, i: (0, 0)) for c in consts]
    if reverse:
        def blk(b, i):
            return b * nt_b + jnp.where(i == 0, 0, nt_b - i)
    else:
        def blk(b, i):
            return b * nt_b + i
    qs = pl.BlockSpec((TM, B_QK), lambda b, i: (blk(b, i), 0))
    ks = pl.BlockSpec((TM, B_QK), lambda b, i: (blk(b, i), 1))
    vs = pl.BlockSpec((TM, B_V), lambda b, i: (blk(b, i), 1))
    rsp = pl.BlockSpec((TM, B_V), lambda b, i: (blk(b, i), 2))
    asp = pl.BlockSpec((TM, LANES), lambda b, i: (blk(b, i), (2 * B_QK + 2 * B_V) // LANES))
    osp = pl.BlockSpec((TM, B_V), lambda b, i: (blk(b, i), 0))
    wsp = pl.BlockSpec((LANES, 2 * B_QK), lambda b, i: (0, 0))
    bsp = pl.BlockSpec((1, B_QK), lambda b, i: (0, 0))
    if reverse:
        in_specs = [qs, ks, vs, asp, rsp, osp, wsp, bsp] + const_specs + [
            pl.BlockSpec((1, GLA_DV), lambda b, i: (0, 0))]
        args = (zb, zb, zb, zb, zb, o_fwd, wg, bg) + consts + (ng,)
        out_dtype = BF16
    else:
        in_specs = [qs, ks, vs, asp, wsp, bsp] + const_specs
        args = (zb, zb, zb, zb, wg, bg) + consts
        out_dtype = F32
    return pl.pallas_call(
        functools.partial(_gla_kernel, reverse=reverse),
        grid=(bsz, nt_b),
        in_specs=in_specs,
        out_specs=osp,
        out_shape=jax.ShapeDtypeStruct((t, B_V), out_dtype),
        scratch_shapes=[pltpu.VMEM((GLA_DV, B_QK), F32)],
        compiler_params=_cparams(("arbitrary", "arbitrary")),
        name="gla_bwd" if reverse else "gla_fwd",
    )(*args)


def _na_kernel(q_ref, k_ref, v_ref, bias_ref, o_ref, *, n_rows, ctx_len):
    i = pl.program_id(1)
    W = GRID_W
    rows_per_step = q_ref.shape[0] // W
    n_ctx_steps = ctx_len // q_ref.shape[0]
    lo_half = lax.broadcasted_iota(jnp.int32, (1, LANES), 1) < HEAD_DIM
    scale = HEAD_DIM ** -0.5
    n_pairs = C_W // LANES

    def pair_queries(q0, nq, pr):
        qp = q_ref[q0:q0 + nq, pr * LANES:(pr + 1) * LANES]
        zero = jnp.zeros_like(qp)
        return jnp.concatenate([jnp.where(lo_half, qp, zero), jnp.where(lo_half, zero, qp)], axis=0)

    def pair_out(o, nq):
        return jnp.where(lo_half, o[0:nq, :], o[nq:2 * nq, :])

    @pl.when(i < n_ctx_steps)
    def _ctx_queries():
        nq = q_ref.shape[0]
        outs = []
        for pr in range(n_pairs):
            cols = slice(pr * LANES, (pr + 1) * LANES)
            s_c = _nt(pair_queries(0, nq, pr), k_ref[0:ctx_len, cols]) * scale
            m = jnp.max(s_c, axis=-1, keepdims=True)
            p = jnp.exp(s_c - m)
            l = jnp.sum(p, axis=-1, keepdims=True)
            outs.append(pair_out(_dot(p.astype(BF16), v_ref[0:ctx_len, cols]) / l, nq))
        o_ref[...] = jnp.concatenate(outs, axis=1).astype(o_ref.dtype)

    @pl.when(i >= n_ctx_steps)
    def _grid_rows():
        for sub in range(rows_per_step):
            r = (i - n_ctx_steps) * rows_per_step + sub
            rs = jnp.clip(r - NA_KH // 2, 0, n_rows - NA_KH)
            start = pl.multiple_of(ctx_len + rs * W, W)
            win = pl.ds(start, NA_KH * W)
            outs = []
            for pr in range(n_pairs):
                cols = slice(pr * LANES, (pr + 1) * LANES)
                qs = pair_queries(sub * W, W, pr)
                s_c = _nt(qs, k_ref[0:ctx_len, cols]) * scale
                s_w = _nt(qs, k_ref[win, cols]) * scale + bias_ref[r - rs, pr * 2 * W:(pr + 1) * 2 * W, :]
                m = jnp.maximum(jnp.max(s_w, axis=-1, keepdims=True), jnp.max(s_c, axis=-1, keepdims=True))
                p_w = jnp.exp(s_w - m)
                p_c = jnp.exp(s_c - m)
                l = jnp.sum(p_w, axis=-1, keepdims=True) + jnp.sum(p_c, axis=-1, keepdims=True)
                o = _dot(p_w.astype(BF16), v_ref[win, cols]) + _dot(p_c.astype(BF16), v_ref[0:ctx_len, cols])
                outs.append(pair_out(o / l, W))
            o_ref[sub * W:(sub + 1) * W, :] = jnp.concatenate(outs, axis=1).astype(o_ref.dtype)


def _na_call(zc, bias_d, bsz, tb):
    t = zc.shape[0]
    q_rows = NA_ROWS_PER_STEP * GRID_W
    steps = tb // q_rows
    n_rows = (tb - TM) // GRID_W
    assert TM % q_rows == 0 and n_rows % NA_ROWS_PER_STEP == 0
    kern = functools.partial(_na_kernel, n_rows=n_rows, ctx_len=TM)
    return pl.pallas_call(
        kern,
        grid=(bsz, steps),
        in_specs=[pl.BlockSpec((q_rows, C_W), lambda b, i: (b * steps + i, 0)),
                  pl.BlockSpec((tb, C_W), lambda b, i: (b, 1)),
                  pl.BlockSpec((tb, C_W), lambda b, i: (b, 2)),
                  pl.BlockSpec(bias_d.shape, lambda b, i: (0, 0, 0))],
        out_specs=pl.BlockSpec((q_rows, C_W), lambda b, i: (b * steps + i, 0)),
        out_shape=jax.ShapeDtypeStruct((t, C_W), BF16),
        compiler_params=_cparams(("arbitrary", "arbitrary")),
        name="nbr_attn",
    )(zc, zc, zc, bias_d)


def _merge_kernel(x_ref, mod_ref, oa_ref, ob_ref, oc_ref, wg_ref, bg_ref, wa_ref, wb_ref, wc_ref,
                  wo_ref, lg_ref, lb_ref, wr_ref, br_ref, x1_ref, h2_ref, rt_ref, *, alpha):
    d = x_ref.shape[1]
    x = x_ref[...]
    h = _modulate(x, mod_ref[0, 0:1, :], mod_ref[0, 1:2, :]).astype(BF16)
    y = jnp.zeros_like(x)
    for bi, (o_ref, w_ref) in enumerate(((oa_ref, wa_ref), (ob_ref, wb_ref), (oc_ref, wc_ref))):
        gate = jax.nn.sigmoid(_dot(h, wg_ref[:, bi * d:(bi + 1) * d]) + bg_ref[:, bi * d:(bi + 1) * d])
        y = y + gate * _dot(o_ref[...], w_ref[...])
    y2 = _dot(y.astype(BF16), wo_ref[...])
    x1 = _ln0(alpha * x + mod_ref[0, 2:3, :] * y2) * lg_ref[...] + lb_ref[...]
    x1_ref[...] = x1
    h2 = _modulate(x1, mod_ref[0, 3:4, :], mod_ref[0, 4:5, :])
    h2_ref[...] = h2

    h2_hi = h2.astype(BF16)
    h2_lo = (h2 - h2_hi.astype(F32)).astype(BF16)
    hw = _dot(h2_hi, wr_ref[...])
    logits = hw[:, 0:LANES] + hw[:, LANES:2 * LANES] + _dot(h2_lo, wr_ref[:, 0:LANES]) + br_ref[...]
    lane = lax.broadcasted_iota(jnp.int32, logits.shape, 1)
    big = jnp.int32(1 << 20)
    neg = jnp.float32(-jnp.inf)

    def amax(vals):
        m = jnp.max(vals, axis=-1, keepdims=True)
        return m, jnp.min(jnp.where(vals == m, lane, big), axis=-1, keepdims=True)

    gl = jnp.where(lane < N_GROUPS, logits, neg)
    gmax, grp = amax(gl)
    p_grp = 1.0 / jnp.sum(jnp.exp(gl - gmax), axis=-1, keepdims=True)
    lo = N_GROUPS + grp * EXPERTS_PER_GROUP
    sl = jnp.where((lane >= lo) & (lane < lo + EXPERTS_PER_GROUP), logits, neg)
    v1, i1 = amax(sl)
    v2, i2 = amax(jnp.where(lane == i1, neg, sl))
    e2 = jnp.exp(v2 - v1)
    g1 = p_grp / (1.0 + e2)
    g2 = p_grp * e2 / (1.0 + e2)
    rt = jnp.where(lane == 0, (i1 - N_GROUPS).astype(F32),
                   jnp.where(lane == 1, (i2 - N_GROUPS).astype(F32),
                             jnp.where(lane == 2, g1, jnp.where(lane == 3, g2, 0.0))))
    rt_ref[...] = rt


def _merge_call(xs, mods, oa, ob, oc, wgate, bgate, wa, wb, wc, wo, lg, lb, wr, br, nt_b, alpha):
    t, d = xs.shape
    full = lambda shape: pl.BlockSpec(shape, lambda j: tuple(0 for _ in shape))
    row = lambda w: pl.BlockSpec((TM, w), lambda j: (j, 0))
    return pl.pallas_call(
        functools.partial(_merge_kernel, alpha=alpha),
        grid=(t // TM,),
        in_specs=[row(d), pl.BlockSpec((1, 6, d), _mod_index(nt_b, t // TM)), row(A_Q), row(B_V), row(C_W),
                  full(wgate.shape), full(bgate.shape), full(wa.shape), full(wb.shape), full(wc.shape),
                  full(wo.shape), full(lg.shape), full(lb.shape), full(wr.shape), full(br.shape)],
        out_specs=[row(d), row(d), row(LANES)],
        out_shape=[jax.ShapeDtypeStruct((t, d), F32), jax.ShapeDtypeStruct((t, d), F32),
                   jax.ShapeDtypeStruct((t, LANES), F32)],
        compiler_params=_cparams(("arbitrary",)),
        name="merge_router",
    )(xs, mods, oa, ob, oc, wgate, bgate, wa, wb, wc, wo, lg, lb, wr, br)


def _dispatch_kernel(pos_ref, h_ref, xs_in_ref, xs_ref, sem):
    del xs_in_ref

    def row_copy(tok, slot):
        return pltpu.make_async_copy(h_ref.at[pl.ds(tok, 1)], xs_ref.at[pl.ds(slot, 1)], sem)

    def issue(tok, carry):
        for kk in range(TOP_K):
            row_copy(tok, pos_ref[0, 0, TOP_K * tok + kk]).start()
        return carry

    lax.fori_loop(0, TM, issue, 0, unroll=DMA_UNROLL)
    for kk in range(TOP_K):
        pltpu.make_async_copy(h_ref, xs_ref.at[pl.ds(0, TM)], sem).wait()


def _dispatch_call(pos3, h2, n_slots, prev_slots):
    t, d = h2.shape
    zeros = jnp.zeros((n_slots, d), F32) if prev_slots is None else prev_slots
    return pl.pallas_call(
        _dispatch_kernel,
        grid=(t // TM,),
        in_specs=[pl.BlockSpec((1, 1, TOP_K * TM), lambda j: (j, 0, 0), memory_space=pltpu.SMEM),
                  pl.BlockSpec((TM, d), lambda j: (j, 0)),
                  pl.BlockSpec(memory_space=pl.ANY)],
        out_specs=pl.BlockSpec(memory_space=pl.ANY),
        out_shape=jax.ShapeDtypeStruct((n_slots, d), F32),
        scratch_shapes=[pltpu.SemaphoreType.DMA(())],
        input_output_aliases={2: 0},
        compiler_params=_cparams(("arbitrary",)),
        name="moe_dispatch",
    )(pos3, h2, zeros)


def _expert_kernel(be_ref, x_ref, w1_ref, w3_ref, w2_ref, y_ref):
    del be_ref
    xb = x_ref[...].astype(BF16)
    a = _dot(xb, w1_ref[0].astype(BF16))
    b = _dot(xb, w3_ref[0].astype(BF16))
    mid = (a * jax.nn.sigmoid(a) * b).astype(BF16)
    y_ref[...] = _dot(mid, w2_ref[0].astype(BF16))


def _expert_call(block_expert, x_slots, w1, w3, w2):
    n_slots, d = x_slots.shape
    de = w1.shape[2]
    grid_spec = pltpu.PrefetchScalarGridSpec(
        num_scalar_prefetch=1,
        grid=(n_slots // MOE_BLK,),
        in_specs=[pl.BlockSpec((MOE_BLK, d), lambda j, be: (j, 0)),
                  pl.BlockSpec((1, d, de), lambda j, be: (be[j], 0, 0)),
                  pl.BlockSpec((1, d, de), lambda j, be: (be[j], 0, 0)),
                  pl.BlockSpec((1, de, d), lambda j, be: (be[j], 0, 0))],
        out_specs=pl.BlockSpec((MOE_BLK, d), lambda j, be: (j, 0)),
    )
    return pl.pallas_call(
        _expert_kernel,
        grid_spec=grid_spec,
        out_shape=jax.ShapeDtypeStruct((n_slots, d), F32),
        compiler_params=_cparams(("arbitrary",)),
        name="moe_experts",
    )(block_expert, x_slots, w1, w3, w2)


def _combine_kernel(pos_ref, x1_ref, mod_ref, rt_ref, ys_ref, lg_ref, lb_ref, o_ref, buf, sem, *, alpha):
    def row_copy(slot, kk, tok):
        return pltpu.make_async_copy(ys_ref.at[pl.ds(slot, 1)], buf.at[kk, pl.ds(tok, 1)], sem)

    def issue(tok, carry):
        for kk in range(TOP_K):
            row_copy(pos_ref[0, 0, TOP_K * tok + kk], kk, tok).start()
        return carry

    lax.fori_loop(0, TM, issue, 0, unroll=DMA_UNROLL)
    for kk in range(TOP_K):
        pltpu.make_async_copy(ys_ref.at[pl.ds(0, TM)], buf.at[kk], sem).wait()
    rt = rt_ref[...]
    y = rt[:, 2:3] * buf[0] + rt[:, 3:4] * buf[1]
    x2 = _ln0(alpha * x1_ref[...] + mod_ref[0, 5:6, :] * y) * lg_ref[...] + lb_ref[...]
    o_ref[...] = x2


def _combine_call(pos3, x1, mods, rt, y_slots, lg, lb, nt_b, alpha, latent_only):
    t, d = x1.shape
    vec = pl.BlockSpec((1, d), lambda j: (0, 0))
    if latent_only:
        n_lat = nt_b - 1
        n_out = (t // TM // nt_b) * n_lat
        tile = lambda j: (j // n_lat) * nt_b + 1 + j % n_lat
        mod_idx = lambda j: (j // n_lat, 0, 0)
    else:
        n_out = t // TM
        tile = lambda j: j
        mod_idx = _mod_index(nt_b, t // TM)
    return pl.pallas_call(
        functools.partial(_combine_kernel, alpha=alpha),
        grid=(n_out,),
        in_specs=[pl.BlockSpec((1, 1, TOP_K * TM), lambda j: (tile(j), 0, 0), memory_space=pltpu.SMEM),
                  pl.BlockSpec((TM, d), lambda j: (tile(j), 0)),
                  pl.BlockSpec((1, 6, d), mod_idx),
                  pl.BlockSpec((TM, LANES), lambda j: (tile(j), 0)),
                  pl.BlockSpec(memory_space=pl.ANY), vec, vec],
        out_specs=pl.BlockSpec((TM, d), lambda j: (j, 0)),
        out_shape=jax.ShapeDtypeStruct((n_out * TM, d), F32),
        scratch_shapes=[pltpu.VMEM((TOP_K, TM, d), F32), pltpu.SemaphoreType.DMA(())],
        compiler_params=_cparams(("arbitrary",)),
        name="moe_combine",
    )(pos3, x1, mods, rt, y_slots, lg, lb)


def _slot_plan(rt, n_blocks):
    flat_e = rt[:, 0:TOP_K].astype(jnp.int32).reshape(-1)
    onehot = (flat_e[:, None] == jnp.arange(N_EXPERTS, dtype=jnp.int32)[None, :]).astype(jnp.int32)
    csum = jnp.cumsum(onehot, axis=0)
    counts = csum[-1]
    padded = ((counts + MOE_BLK - 1) // MOE_BLK) * MOE_BLK
    pad_end = jnp.cumsum(padded)
    pad_start = pad_end - padded
    pos = jnp.sum(onehot * (csum - 1 + pad_start[None, :]), axis=1)
    block_start = jnp.arange(n_blocks, dtype=jnp.int32) * MOE_BLK
    block_expert = jnp.minimum(
        jnp.sum((pad_end[None, :] <= block_start[:, None]).astype(jnp.int32), axis=1), N_EXPERTS - 1)
    return pos.astype(jnp.int32), block_expert.astype(jnp.int32)


def _rope_tables(seq_len, ctx_len):
    n_freq = HEAD_DIM // 4
    t = jnp.arange(seq_len)
    invf = ROPE_THETA ** (-jnp.arange(n_freq, dtype=F32) / n_freq)
    ar = (t // GRID_W).astype(F32)[:, None] * invf
    ac = (t % GRID_W).astype(F32)[:, None] * invf
    ang = jnp.concatenate([ar, ar, ac, ac], axis=-1)
    cos, sin = jnp.cos(ang), jnp.sin(ang)
    quarter = (np.arange(HEAD_DIM) // n_freq) % 2
    sa = sin * jnp.asarray(quarter == 1, F32)
    sb = -sin * jnp.asarray(quarter == 0, F32)
    ones = jnp.ones((ctx_len, HEAD_DIM), F32)
    zeros = jnp.zeros((ctx_len, HEAD_DIM), F32)
    tabs = []
    for lat, ctx in ((cos, ones), (sa, zeros), (sb, zeros)):
        tab = jnp.concatenate([ctx, lat], axis=0)
        tabs.append(jnp.tile(tab, (1, LANES // HEAD_DIM)))
    return tabs


def _na_bias_table(rpb):
    kh = NA_KH
    w = np.arange(GRID_W)
    col_start = np.clip(w - NA_KW // 2, 0, GRID_W - NA_KW)
    colk = np.arange(GRID_W)
    inside = (colk[None, :] >= col_start[:, None]) & (colk[None, :] < col_start[:, None] + NA_KW)
    col_rel = colk[None, :] - w[:, None] + (NA_KW - 1)
    pats = np.arange(kh)
    row_rel = np.arange(kh)[None, :] - pats[:, None] + (NA_KH - 1)
    row_sel = (row_rel[:, :, None] == np.arange(2 * NA_KH - 1)).astype(np.float32)
    col_sel = ((col_rel[:, :, None] == np.arange(2 * NA_KW - 1)) & inside[:, :, None]).astype(np.float32)
    b = jnp.einsum('lhab,pia,wcb->lphwic', rpb, jnp.asarray(row_sel), jnp.asarray(col_sel), precision=HIGHEST)
    b = jnp.where(jnp.asarray(inside)[None, None, None, :, None, :], b, NEG_BIG)
    return b.reshape(rpb.shape[0], kh, NA_HEADS * GRID_W, kh * GRID_W).astype(F32)


def _pack_in_proj(w_in, b_in):
    a0 = 0
    a1 = a0 + NCOL_A
    bq0 = a1
    b_end = bq0 + 2 * B_QK + 2 * B_V + B_A
    c_end = b_end + NCOL_C
    d = w_in.shape[0]
    padw = jnp.zeros((d, LANES - B_A), w_in.dtype)
    padb = jnp.zeros((LANES - B_A,), b_in.dtype)
    w_p = jnp.concatenate([w_in[:, a0:a1], w_in[:, bq0:b_end], padw, w_in[:, b_end:c_end]], axis=1)
    b_p = jnp.concatenate([b_in[a0:a1], b_in[bq0:b_end], padb, b_in[b_end:c_end]])
    return w_p.astype(BF16), b_p.reshape(1, -1), w_in[:, c_end:].astype(BF16), b_in[c_end:].reshape(1, -1)


def kernel(x, c, ctx, c_ctx, w_mod, b_mod, w_in, b_in, attn_q_norm, attn_k_norm, gla_w_gate, gla_b_gate,
           gla_norm, na_rpb, w_br_attn, w_br_gla, w_br_na, w_out, ln1_g, ln1_b, w_router_group,
           b_router_group, w_router_expert, b_router_expert, moe_w1, moe_w3, moe_w2, ln2_g, ln2_b):
    bsz, seq, d = x.shape
    ctx_len = ctx.shape[1]
    depth = w_mod.shape[0]
    assert ctx_len == TM and seq % TM == 0 and (seq // GRID_W) >= NA_KH
    tb = ctx_len + seq
    nt_b = tb // TM
    t = bsz * tb
    alpha = (2 * depth) ** 0.25

    xs = jnp.concatenate([ctx, x], axis=1).reshape(t, d)

    cin = jnp.zeros((8, d), F32).at[0:bsz].set(c).at[bsz].set(c_ctx)
    mod_all = _mod_call(cin, w_mod, b_mod)

    tabs = _rope_tables(seq, ctx_len)
    gm = jnp.asarray(np.kron(np.eye(LANES // HEAD_DIM), np.ones((HEAD_DIM, HEAD_DIM))) / HEAD_DIM, BF16)
    n_assign = t * TOP_K
    n_blocks = -(-n_assign // MOE_BLK) + N_EXPERTS
    n_slots = n_blocks * MOE_BLK
    na_bias = _na_bias_table(na_rpb)
    w1_all = moe_w1.reshape((depth * N_EXPERTS,) + moe_w1.shape[2:])
    w3_all = moe_w3.reshape((depth * N_EXPERTS,) + moe_w3.shape[2:])
    w2_all = moe_w2.reshape((depth * N_EXPERTS,) + moe_w2.shape[2:])

    x_slots = None
    for l in range(depth):
        m = mod_all[l].reshape(8, 6, d)
        mods = m

        w_p, b_p, w_gate_cols, b_gate_cols = _pack_in_proj(w_in[l], b_in[l])
        qn = jnp.tile(attn_q_norm[l], LANES // HEAD_DIM).reshape(1, LANES)
        kn = jnp.tile(attn_k_norm[l], LANES // HEAD_DIM).reshape(1, LANES)
        za, zb, zc = _proj_call(xs, mods, w_p, b_p, tabs, qn, kn, gm, nt_b)
        oa = _gqa_call(za, bsz, tb)

        o_f = None
        for direction in range(2):
            wg = jnp.zeros((LANES, B_QK), F32).at[direction * GLA_GATE_RANK:(direction + 1) * GLA_GATE_RANK].set(
                gla_w_gate[l, direction])
            wg_hi = wg.astype(BF16)
            wg = jnp.concatenate([wg_hi, (wg - wg_hi.astype(F32)).astype(BF16)], axis=1)
            bg = gla_b_gate[l, direction].reshape(1, B_QK)
            o_f = _gla_call(zb, o_f, wg, bg, gla_norm[l].reshape(1, GLA_DV), bsz, tb, reverse=direction == 1)
        ob = o_f

        oc = _na_call(zc, na_bias[l], bsz, tb)

        wr = jnp.concatenate([w_router_group[l], w_router_expert[l],
                              jnp.zeros((d, LANES - N_GROUPS - N_EXPERTS), F32)], axis=1)
        br = jnp.concatenate([b_router_group[l], b_router_expert[l],
                              jnp.zeros((LANES - N_GROUPS - N_EXPERTS,), F32)]).reshape(1, LANES)
        wr_hi = wr.astype(BF16)
        wr = jnp.concatenate([wr_hi, (wr - wr_hi.astype(F32)).astype(BF16)], axis=1)
        x1, h2, rt = _merge_call(xs, mods, oa, ob, oc, w_gate_cols, b_gate_cols,
                                 w_br_attn[l].astype(BF16), w_br_gla[l].astype(BF16), w_br_na[l].astype(BF16),
                                 w_out[l].astype(BF16), ln1_g[l].reshape(1, d), ln1_b[l].reshape(1, d),
                                 wr, br, nt_b, alpha)

        pos, block_expert = _slot_plan(rt, n_blocks)
        pos3 = pos.reshape(t // TM, 1, TOP_K * TM)
        x_slots = _dispatch_call(pos3, h2, n_slots, x_slots)
        y_slots = _expert_call(block_expert + l * N_EXPERTS, x_slots, w1_all, w3_all, w2_all)
        xs = _combine_call(pos3, x1, mods, rt, y_slots, ln2_g[l].reshape(1, d), ln2_b[l].reshape(1, d),
                           nt_b, alpha, latent_only=l == depth - 1)

    return xs.reshape(bsz, seq, d)
```
